```python
import jax, jax.numpy as jnp
from jax import lax
import numpy as np

D_MODEL = 1024
BATCH = 32
SEQ = 2048
DEPTH = 4

CTX_LEN = 256
GRID_W = 64
N_EVEN = (DEPTH + 1) // 2
N_ODD = DEPTH // 2
RET_HEADS = 4
RET_DK = 128
RET_DV = 128
RET_CHUNK = 128
ATT_HEADS = 4
ATT_KV_HEADS = 2
ATT_HD = 128
Q_BLOCK = 128
ROPE_BASE = 10000.0
CM_WIDTH = 1024
CM_GROUPS = 8
CM_GROUP_DIM = CM_WIDTH // CM_GROUPS
CM_CHUNK = 128
FF_HIDDEN = 4 * D_MODEL
EPS = 1e-6
AB_SIZES = (RET_HEADS * RET_DK, RET_HEADS * RET_DK, RET_HEADS * RET_DV, RET_HEADS * RET_DV,
            ATT_HEADS * ATT_HD, ATT_KV_HEADS * ATT_HD, ATT_KV_HEADS * ATT_HD)
AB_IN_W = sum(AB_SIZES)
AB_OUT_W = RET_HEADS * RET_DV + ATT_HEADS * ATT_HD

kernel_name = 'hybrid_retention_gqa_chunkmlp_dit'


def rms_norm(x, g):
    xf = x.astype(jnp.float32)
    y = xf * lax.rsqrt(jnp.mean(xf * xf, axis=-1, keepdims=True) + EPS)
    return (y * g.astype(jnp.float32)).astype(x.dtype)


def modulate(h, shift, scale):
    return h * (1.0 + scale) + shift


def grid_rope(L, hd):
    rows = L // GRID_W
    row = jnp.repeat(jnp.arange(rows, dtype=jnp.float32), GRID_W)
    col = jnp.tile(jnp.arange(GRID_W, dtype=jnp.float32), rows)
    n_freq = hd // 4
    inv = ROPE_BASE ** (-jnp.arange(n_freq, dtype=jnp.float32) / n_freq)
    ang = jnp.concatenate([row[:, None] * inv[None, :], col[:, None] * inv[None, :]], axis=-1)
    return jnp.cos(ang), jnp.sin(ang)


def apply_rope(x, cos, sin):
    half = x.shape[-1] // 2
    x1, x2 = x[..., :half], x[..., half:]
    cs, sn = cos[None, :, None, :], sin[None, :, None, :]
    out = jnp.concatenate([x1 * cs - x2 * sn, x1 * sn + x2 * cs], axis=-1)
    return out.astype(x.dtype)


def split_heads(t, n, d):
    return t.reshape(t.shape[0], t.shape[1], n, d)


def split_ab(p):
    out, start = [], 0
    for s in AB_SIZES:
        out.append(p[..., start:start + s])
        start += s
    return out


def retention_chunkwise(q, k, v, log_gamma, state0):
    B, L, H, dk = q.shape
    dv = v.shape[-1]
    C = RET_CHUNK
    N = L // C
    qc = q.reshape(B, N, C, H, dk)
    kc = k.reshape(B, N, C, H, dk)
    vc = v.reshape(B, N, C, H, dv)
    pos = jnp.arange(C, dtype=jnp.float32)
    rel = pos[:, None] - pos[None, :]
    decay = jnp.where((rel >= 0)[None], jnp.exp(jnp.maximum(rel, 0.0)[None] * log_gamma[:, None, None]), 0.0)
    scores = jnp.einsum('bnihd,bnjhd->bnhij', qc, kc) * decay
    intra = jnp.einsum('bnhij,bnjhe->bnihe', scores, vc)
    k_decay = jnp.exp((C - 1.0 - pos)[:, None] * log_gamma[None, :])
    q_decay = jnp.exp((pos + 1.0)[:, None] * log_gamma[None, :])
    chunk_decay = jnp.exp(C * log_gamma)[None, :, None, None]
    chunk_kv = jnp.einsum('bnjhd,jh,bnjhe->nbhde', kc, k_decay, vc)

    def step(state, kv_n):
        return chunk_decay * state + kv_n, state

    state_final, state_prev = lax.scan(step, state0, chunk_kv)
    cross = jnp.einsum('bnihd,nbhde->bnihe', qc, state_prev) * q_decay[None, None, :, :, None]
    return (intra + cross).reshape(B, L, H, dv), state_final


def retention_out(o, g):
    B, L, H, dv = o.shape
    of = o.astype(jnp.float32)
    mu = jnp.mean(of, axis=-1, keepdims=True)
    var = jnp.mean(jnp.square(of - mu), axis=-1, keepdims=True)
    y = ((of - mu) * lax.rsqrt(var + EPS)).reshape(B, L, H * dv)
    return y.astype(g.dtype) * jax.nn.silu(g)


def block_attention(q, k, v):
    B, Lq, H, hd = q.shape
    KV = k.shape[2]
    G = H // KV
    nb = Lq // Q_BLOCK
    scale = hd ** -0.5
    qb = q.reshape(B, nb, Q_BLOCK, KV, G, hd).transpose(1, 0, 2, 3, 4, 5)

    def one_block(qi):
        s = jnp.einsum('bqkgd,bskd->bkgqs', qi, k).astype(jnp.float32) * scale
        p = jax.nn.softmax(s, axis=-1)
        return jnp.einsum('bkgqs,bskd->bqkgd', p.astype(v.dtype), v)

    o = lax.map(one_block, qb)
    return o.transpose(1, 0, 2, 3, 4, 5).reshape(B, Lq, H * hd)


def mix_ab(h_lat, h_ctx, w_in, w_out, ret_decay, q_g, k_g):
    B, L, _ = h_lat.shape
    cos_r, sin_r = grid_rope(L, RET_DK)
    cos_a, sin_a = grid_rope(L, ATT_HD)
    lat = split_ab(h_lat @ w_in)
    ctx = split_ab(h_ctx @ w_in)
    flip = lambda t: jnp.flip(t, axis=1)
    rq_l = apply_rope(split_heads(lat[0], RET_HEADS, RET_DK), cos_r, sin_r)
    rk_l = apply_rope(split_heads(lat[1], RET_HEADS, RET_DK), cos_r, sin_r) * (RET_DK ** -0.5)
    rv_l = split_heads(lat[2], RET_HEADS, RET_DV)
    rq_c = split_heads(ctx[0], RET_HEADS, RET_DK)
    rk_c = split_heads(ctx[1], RET_HEADS, RET_DK) * (RET_DK ** -0.5)
    rv_c = split_heads(ctx[2], RET_HEADS, RET_DV)
    log_gamma = jax.nn.log_sigmoid(ret_decay.astype(jnp.float32))
    zero = jnp.zeros((B, RET_HEADS, RET_DK, RET_DV), jnp.float32)
    oc_f, st_f = retention_chunkwise(rq_c, rk_c, rv_c, log_gamma[0], zero)
    ol_f, _ = retention_chunkwise(rq_l, rk_l, rv_l, log_gamma[0], st_f)
    oc_b, st_b = retention_chunkwise(flip(rq_c), flip(rk_c), flip(rv_c), log_gamma[1], zero)
    ol_b, _ = retention_chunkwise(flip(rq_l), flip(rk_l), flip(rv_l), log_gamma[1], st_b)
    ret_l = retention_out(ol_f + flip(ol_b), lat[3])
    ret_c = retention_out(oc_f + flip(oc_b), ctx[3])
    aq_l = apply_rope(rms_norm(split_heads(lat[4], ATT_HEADS, ATT_HD), q_g), cos_a, sin_a)
    ak_l = apply_rope(rms_norm(split_heads(lat[5], ATT_KV_HEADS, ATT_HD), k_g), cos_a, sin_a)
    av_l = split_heads(lat[6], ATT_KV_HEADS, ATT_HD)
    aq_c = rms_norm(split_heads(ctx[4], ATT_HEADS, ATT_HD), q_g)
    ak_c = rms_norm(split_heads(ctx[5], ATT_KV_HEADS, ATT_HD), k_g)
    av_c = split_heads(ctx[6], ATT_KV_HEADS, ATT_HD)
    att_l = block_attention(aq_l, jnp.concatenate([ak_c, ak_l], axis=1), jnp.concatenate([av_c, av_l], axis=1))
    att_c = block_attention(aq_c, ak_c, av_c)
    out_l = jnp.concatenate([ret_l, att_l], axis=-1) @ w_out
    out_c = jnp.concatenate([ret_c, att_c], axis=-1) @ w_out
    return out_l, out_c


def mix_chunk_mlp(h, w_in, v_g, w_s, b_s, w_out):
    B, L, _ = h.shape
    z = jax.nn.gelu(h @ w_in)
    u, v = z[..., :CM_WIDTH], z[..., CM_WIDTH:]
    v = rms_norm(v, v_g)
    vc = v.reshape(B, L // CM_CHUNK, CM_CHUNK, CM_GROUPS, CM_GROUP_DIM)
    sv = jnp.einsum('gpq,bnqgd->bnpgd', w_s, vc) + b_s.T[None, None, :, :, None]
    return (u * sv.reshape(B, L, CM_WIDTH)) @ w_out


def sq_relu_mlp(h, w1, w2):
    return jnp.square(jax.nn.relu(h @ w1)) @ w2


def setup_inputs(seed: int = 0) -> dict:
    key = jax.random.key(seed)
    ks = jax.random.split(key, 24)
    f32 = jnp.float32

    def nrm(k, shape, scale):
        return jax.random.normal(k, shape, f32) * scale

    def gain(k, shape):
        return 1.0 + 0.01 * jax.random.normal(k, shape, f32)

    base = 1.0 - 2.0 ** (-5.0 - jnp.arange(RET_HEADS, dtype=f32))
    logit = jnp.log(base) - jnp.log1p(-base)
    ret_decay = logit[None, None, :] + 0.05 * jax.random.normal(ks[10], (N_EVEN, 2, RET_HEADS), f32)
    return {
        'x': nrm(ks[0], (BATCH, SEQ, D_MODEL), 1.0),
        'c': nrm(ks[1], (BATCH, D_MODEL), 1.0),
        'ctx': nrm(ks[2], (BATCH, CTX_LEN, D_MODEL), 1.0),
        'c_ctx': nrm(ks[3], (D_MODEL,), 1.0),
        'mod_w': nrm(ks[4], (DEPTH, D_MODEL, 6 * D_MODEL), D_MODEL ** -0.5),
        'mod_b': nrm(ks[5], (DEPTH, 6 * D_MODEL), 0.01),
        'norm1_g': gain(ks[6], (DEPTH, D_MODEL)),
        'norm2_g': gain(ks[7], (DEPTH, D_MODEL)),
        'ab_w_in': nrm(ks[8], (N_EVEN, D_MODEL, AB_IN_W), D_MODEL ** -0.5),
        'ab_w_out': nrm(ks[9], (N_EVEN, AB_OUT_W, D_MODEL), AB_OUT_W ** -0.5),
        'ret_decay': ret_decay,
        'att_q_norm_g': gain(ks[11], (N_EVEN, ATT_HD)),
        'att_k_norm_g': gain(ks[12], (N_EVEN, ATT_HD)),
        'cm_w_in': nrm(ks[13], (N_ODD, D_MODEL, 2 * CM_WIDTH), D_MODEL ** -0.5),
        'cm_v_norm_g': gain(ks[14], (N_ODD, CM_WIDTH)),
        'cm_w_s': nrm(ks[15], (N_ODD, CM_GROUPS, CM_CHUNK, CM_CHUNK), CM_CHUNK ** -0.5),
        'cm_b_s': gain(ks[16], (N_ODD, CM_GROUPS, CM_CHUNK)),
        'cm_w_out': nrm(ks[17], (N_ODD, CM_WIDTH, D_MODEL), CM_WIDTH ** -0.5),
        'ff_w1': nrm(ks[18], (DEPTH, D_MODEL, FF_HIDDEN), D_MODEL ** -0.5),
        'ff_w2': nrm(ks[19], (DEPTH, FF_HIDDEN, D_MODEL), FF_HIDDEN ** -0.5),
    }


def reference(x, c, ctx, c_ctx, mod_w, mod_b, norm1_g, norm2_g, ab_w_in, ab_w_out, ret_decay,
              att_q_norm_g, att_k_norm_g, cm_w_in, cm_v_norm_g, cm_w_s, cm_b_s, cm_w_out, ff_w1, ff_w2):
    silu_c = jax.nn.silu(c)
    silu_cc = jax.nn.silu(c_ctx)
    h_stream = ctx
    for l in range(DEPTH):
        last = l == DEPTH - 1
        is_even = l % 2 == 0
        i = l // 2
        mod_lat = (silu_c @ mod_w[l] + mod_b[l])[:, None, :]
        mod_ctx = silu_cc @ mod_w[l] + mod_b[l]
        sh1, sc1, g1, sh2, sc2, g2 = jnp.split(mod_lat, 6, axis=-1)
        csh1, csc1, cg1, csh2, csc2, cg2 = jnp.split(mod_ctx, 6, axis=-1)
        h_lat = modulate(rms_norm(x, norm1_g[l]), sh1, sc1)
        if is_even or not last:
            h_ctx = modulate(rms_norm(h_stream, norm1_g[l]), csh1, csc1)
        if is_even:
            o_lat, o_ctx = mix_ab(h_lat, h_ctx, ab_w_in[i], ab_w_out[i], ret_decay[i],
                                  att_q_norm_g[i], att_k_norm_g[i])
        else:
            o_lat = mix_chunk_mlp(h_lat, cm_w_in[i], cm_v_norm_g[i], cm_w_s[i], cm_b_s[i], cm_w_out[i])
            if not last:
                o_ctx = mix_chunk_mlp(h_ctx, cm_w_in[i], cm_v_norm_g[i], cm_w_s[i], cm_b_s[i], cm_w_out[i])
        x = x + g1 * o_lat
        x = x + g2 * sq_relu_mlp(modulate(rms_norm(x, norm2_g[l]), sh2, sc2), ff_w1[l], ff_w2[l])
        if not last:
            h_stream = h_stream + cg1 * o_ctx
            h_stream = h_stream + cg2 * sq_relu_mlp(modulate(rms_norm(h_stream, norm2_g[l]), csh2, csc2),
                                                     ff_w1[l], ff_w2[l])
    return x
```

```python
import functools
import math

import jax
import jax.numpy as jnp
from jax import lax
from jax.experimental import pallas as pl
from jax.experimental.pallas import tpu as pltpu

F32 = jnp.float32
BF16 = jnp.bfloat16

EPS = 1e-6
HEAD_DIM = 128
RET_HEADS = 4
ATT_HEADS = 4
ATT_KV_HEADS = 2
ATT_GROUP = ATT_HEADS // ATT_KV_HEADS
CHUNK = 128
CM_GROUPS = 8
GRID_W = 64
ROPE_BASE = 10000.0
OFF_RQ, OFF_RK, OFF_RV, OFF_RG = 0, RET_HEADS, 2 * RET_HEADS, 3 * RET_HEADS
OFF_AQ = 4 * RET_HEADS
OFF_AK = OFF_AQ + ATT_HEADS
OFF_AV = OFF_AK + ATT_KV_HEADS
AB_HEADS = OFF_AV + ATT_KV_HEADS

V7X_VMEM_LIMIT_BYTES = 56 * 1024 * 1024
ROW_TILE = 512
ATT_Q_TILE = 256
MOD_COL_TILE = 1536


def _params(n_axes):
    return pltpu.CompilerParams(dimension_semantics=("arbitrary",) * n_axes,
                                vmem_limit_bytes=V7X_VMEM_LIMIT_BYTES)


def _resident(shape):
    return pl.BlockSpec(shape, lambda *_: (0,) * len(shape), pipeline_mode=pl.Buffered(1))


def _dot(a, b):
    return jnp.dot(a, b, preferred_element_type=F32)


def _dot_nt(a, b):
    return lax.dot_general(a, b, (((1,), (1,)), ((), ())), preferred_element_type=F32)


def _dot_tn(a, b):
    return lax.dot_general(a, b, (((0,), (0,)), ((), ())), preferred_element_type=F32)


def _rms(x, g):
    return x * lax.rsqrt(jnp.mean(x * x, axis=-1, keepdims=True) + EPS) * g


def _norm_mod(x, g, shift, scale):
    return _rms(x, g) * (1.0 + scale) + shift


def _silu(x):
    return x / (1.0 + jnp.exp(-x))


def _gelu_tanh(x):
    return 0.5 * x * (1.0 + jnp.tanh(math.sqrt(2.0 / math.pi) * (x + 0.044715 * (x * x * x))))


def _mod_kernel(c_ref, w_ref, b_ref, o_ref):
    s = _silu(c_ref[...]).astype(BF16)
    o_ref[...] = _dot(s, w_ref[...].astype(BF16)) + b_ref[...]


def _mod_call(c_rows, mod_w, mod_b):
    depth, d, n = mod_w.shape
    r = c_rows.shape[0]
    tn = MOD_COL_TILE if n % MOD_COL_TILE == 0 else n
    return pl.pallas_call(
        _mod_kernel,
        grid=(depth, n // tn),
        in_specs=[pl.BlockSpec((r, d), lambda l, j: (0, 0)),
                  pl.BlockSpec((None, d, tn), lambda l, j: (l, 0, j)),
                  pl.BlockSpec((None, 1, tn), lambda l, j: (l, 0, j))],
        out_specs=pl.BlockSpec((None, r, tn), lambda l, j: (l, 0, j)),
        out_shape=jax.ShapeDtypeStruct((depth, r, n), F32),
        compiler_params=_params(2),
        name="mod",
    )(c_rows, mod_w, mod_b.reshape(depth, 1, n))


def _proj_kernel(*refs, rope):
    if rope:
        x_ref, mod_ref, g_ref, w_ref, qg_ref, kg_ref, cs_ref, sn_ref, o_ref = refs
        cs, sn = cs_ref[...], sn_ref[...]
    else:
        x_ref, mod_ref, g_ref, w_ref, qg_ref, kg_ref, o_ref = refs
    m = mod_ref[...]
    h = _norm_mod(x_ref[...], g_ref[...], m[0:1], m[1:2]).astype(BF16)

    def rot(t):
        return t * cs + pltpu.roll(t, HEAD_DIM // 2, 1) * sn if rope else t

    def store(head, t):
        o_ref[:, head * HEAD_DIM:(head + 1) * HEAD_DIM] = t.astype(BF16)

    def segment(first, count):
        y = _dot(h, w_ref[:, first * HEAD_DIM:(first + count) * HEAD_DIM])
        return [y[:, i * HEAD_DIM:(i + 1) * HEAD_DIM] for i in range(count)]

    scale = HEAD_DIM ** -0.5
    for i, t in enumerate(segment(OFF_RQ, RET_HEADS)):
        store(OFF_RQ + i, rot(t))
    for i, t in enumerate(segment(OFF_RK, RET_HEADS)):
        store(OFF_RK + i, rot(t) * scale)
    for i, t in enumerate(segment(OFF_RV, RET_HEADS)):
        store(OFF_RV + i, t)
    for i, t in enumerate(segment(OFF_RG, RET_HEADS)):
        store(OFF_RG + i, t)
    for i, t in enumerate(segment(OFF_AQ, ATT_HEADS)):
        store(OFF_AQ + i, rot(_rms(t, qg_ref[...])) * scale)
    for i, t in enumerate(segment(OFF_AK, ATT_KV_HEADS)):
        store(OFF_AK + i, rot(_rms(t, kg_ref[...])))
    for i, t in enumerate(segment(OFF_AV, ATT_KV_HEADS)):
        store(OFF_AV + i, t)


def _proj_call(xs, mod, layer, mod_row, norm_g, w_in, q_g, k_g, rope_tabs):
    b, l, d = xs.shape
    n = w_in.shape[1]
    tm = min(ROW_TILE, l)
    rope = rope_tabs is not None
    in_specs = [pl.BlockSpec((None, tm, d), lambda bi, i: (bi, i, 0)),
                pl.BlockSpec((None, None, 6, d), lambda bi, i: (layer, mod_row(bi), 0, 0)),
                _resident((1, d)), _resident((d, n)),
                _resident((1, HEAD_DIM)), _resident((1, HEAD_DIM))]
    args = [xs, mod, norm_g.reshape(1, d), w_in, q_g.reshape(1, HEAD_DIM), k_g.reshape(1, HEAD_DIM)]
    if rope:
        in_specs += [pl.BlockSpec((tm, HEAD_DIM), lambda bi, i: (i, 0))] * 2
        args += list(rope_tabs)
    return pl.pallas_call(
        functools.partial(_proj_kernel, rope=rope),
        grid=(b, l // tm),
        in_specs=in_specs,
        out_specs=pl.BlockSpec((None, tm, n), lambda bi, i: (bi, i, 0)),
        out_shape=jax.ShapeDtypeStruct((b, l, n), BF16),
        compiler_params=_params(2),
        name="proj_lat" if rope else "proj_ctx",
    )(*args)


def _ret_kernel(lg_ref, ql_ref, kl_ref, vl_ref, gl_ref, qc_ref, kc_ref, vc_ref, gc_ref,
                ol_ref, oc_ref, sfl_ref, sfc_ref):
    c = CHUNK
    n_lat = ql_ref.shape[0] // c
    n_ctx = qc_ref.shape[0] // c
    head = pl.program_id(1)
    lgf = lg_ref[0, head]
    lgb = lg_ref[1, head]
    row = lax.broadcasted_iota(jnp.int32, (c, HEAD_DIM), 0).astype(F32)
    col = lax.broadcasted_iota(jnp.int32, (c, c), 1).astype(F32)
    rel = lax.broadcasted_iota(jnp.int32, (c, c), 0).astype(F32) - col
    dmat = (jnp.where(rel >= 0, jnp.exp(jnp.maximum(rel, 0.0) * lgf), 0.0)
            + jnp.where(rel <= 0, jnp.exp(jnp.maximum(-rel, 0.0) * lgb), 0.0))
    kdf = jnp.exp((c - 1.0 - row) * lgf)
    kdb = jnp.exp(row * lgb)
    qdf = jnp.exp((row + 1.0) * lgf)
    qdb = jnp.exp((c - row) * lgb)
    cdf = jnp.exp(jnp.full((HEAD_DIM, HEAD_DIM), c, F32) * lgf)
    cdb = jnp.exp(jnp.full((HEAD_DIM, HEAD_DIM), c, F32) * lgb)

    def kv_update(state, k, v, k_decay, chunk_decay):
        kd = (k.astype(F32) * k_decay).astype(BF16)
        return chunk_decay * state + _dot_tn(kd, v)

    def emit(q, k, v, gate, sf, sb, out_ref, rows):
        a = (_dot_nt(q, k) * dmat).astype(BF16)
        o = _dot(a, v) + _dot(q, sf) * qdf + _dot(q, sb.astype(BF16)) * qdb
        mu = jnp.mean(o, axis=-1, keepdims=True)
        dev = o - mu
        y = dev * lax.rsqrt(jnp.mean(dev * dev, axis=-1, keepdims=True) + EPS)
        out_ref[rows, :] = (y * _silu(gate.astype(F32))).astype(BF16)

    state = jnp.zeros((HEAD_DIM, HEAD_DIM), F32)
    for n in range(n_ctx):
        rows = slice(n * c, (n + 1) * c)
        sfc_ref[n] = state.astype(BF16)
        state = kv_update(state, kc_ref[rows, :], vc_ref[rows, :], kdf, cdf)

    def fwd_body(n, st):
        rows = pl.ds(pl.multiple_of(n * c, c), c)
        sfl_ref[n] = st.astype(BF16)
        return kv_update(st, kl_ref[rows, :], vl_ref[rows, :], kdf, cdf)

    lax.fori_loop(0, n_lat, fwd_body, state)

    state = jnp.zeros((HEAD_DIM, HEAD_DIM), F32)
    for n in reversed(range(n_ctx)):
        rows = slice(n * c, (n + 1) * c)
        k, v = kc_ref[rows, :], vc_ref[rows, :]
        emit(qc_ref[rows, :], k, v, gc_ref[rows, :], sfc_ref[n], state, oc_ref, rows)
        state = kv_update(state, k, v, kdb, cdb)

    def bwd_body(i, st):
        n = n_lat - 1 - i
        rows = pl.ds(pl.multiple_of(n * c, c), c)
        k, v = kl_ref[rows, :], vl_ref[rows, :]
        emit(ql_ref[rows, :], k, v, gl_ref[rows, :], sfl_ref[n], st, ol_ref, rows)
        return kv_update(st, k, v, kdb, cdb)

    lax.fori_loop(0, n_lat, bwd_body, state)


def _ret_call(p_lat, p_ctx, log_gamma):
    b, l, _ = p_lat.shape
    lc = p_ctx.shape[1]

    def head_spec(rows, off):
        return pl.BlockSpec((None, rows, HEAD_DIM), lambda bi, h: (bi, 0, off + h))

    return pl.pallas_call(
        _ret_kernel,
        grid=(b, RET_HEADS),
        in_specs=[pl.BlockSpec(memory_space=pltpu.SMEM)]
        + [head_spec(l, off) for off in (OFF_RQ, OFF_RK, OFF_RV, OFF_RG)]
        + [head_spec(lc, off) for off in (OFF_RQ, OFF_RK, OFF_RV, OFF_RG)],
        out_specs=[head_spec(l, 0), head_spec(lc, 0)],
        out_shape=[jax.ShapeDtypeStruct((b, l, RET_HEADS * HEAD_DIM), BF16),
                   jax.ShapeDtypeStruct((b, lc, RET_HEADS * HEAD_DIM), BF16)],
        scratch_shapes=[pltpu.VMEM((l // CHUNK, HEAD_DIM, HEAD_DIM), BF16),
                        pltpu.VMEM((lc // CHUNK, HEAD_DIM, HEAD_DIM), BF16)],
        compiler_params=_params(2),
        name="retention",
    )(log_gamma, p_lat, p_lat, p_lat, p_lat, p_ctx, p_ctx, p_ctx, p_ctx)


def _attn_kernel(*refs, with_lat):
    if with_lat:
        q_ref, kc_ref, vc_ref, kl_ref, vl_ref, o_ref = refs
    else:
        q_ref, kc_ref, vc_ref, o_ref = refs
    for g in range(ATT_GROUP):
        cols = slice(g * HEAD_DIM, (g + 1) * HEAD_DIM)
        q = q_ref[:, cols]
        s_c = _dot_nt(q, kc_ref[...])
        m = jnp.max(s_c, axis=-1, keepdims=True)
        if with_lat:
            s_l = _dot_nt(q, kl_ref[...])
            m = jnp.maximum(m, jnp.max(s_l, axis=-1, keepdims=True))
        p_c = jnp.exp(s_c - m)
        den = jnp.sum(p_c, axis=-1, keepdims=True)
        acc = _dot(p_c.astype(BF16), vc_ref[...])
        if with_lat:
            p_l = jnp.exp(s_l - m)
            den = den + jnp.sum(p_l, axis=-1, keepdims=True)
            acc = acc + _dot(p_l.astype(BF16), vl_ref[...])
        o_ref[:, cols] = (acc / den).astype(BF16)


def _attn_call(p_q, p_ctx, p_lat):
    b, lq, _ = p_q.shape
    lc = p_ctx.shape[1]
    tq = min(ATT_Q_TILE, lq)
    with_lat = p_lat is not None
    gw = ATT_GROUP * HEAD_DIM

    def kv_spec(rows, off):
        return pl.BlockSpec((None, rows, HEAD_DIM), lambda bi, kv, i: (bi, 0, off + kv))

    in_specs = [pl.BlockSpec((None, tq, gw), lambda bi, kv, i: (bi, i, OFF_AQ // ATT_GROUP + kv)),
                kv_spec(lc, OFF_AK), kv_spec(lc, OFF_AV)]
    args = [p_q, p_ctx, p_ctx]
    if with_lat:
        in_specs += [kv_spec(p_lat.shape[1], OFF_AK), kv_spec(p_lat.shape[1], OFF_AV)]
        args += [p_lat, p_lat]
    return pl.pallas_call(
        functools.partial(_attn_kernel, with_lat=with_lat),
        grid=(b, ATT_KV_HEADS, lq // tq),
        in_specs=in_specs,
        out_specs=pl.BlockSpec((None, tq, gw), lambda bi, kv, i: (bi, i, kv)),
        out_shape=jax.ShapeDtypeStruct((b, lq, ATT_HEADS * HEAD_DIM), BF16),
        compiler_params=_params(3),
        name="attn_lat" if with_lat else "attn_ctx",
    )(*args)


def _ffn_tail(x1, m, n2g_ref, w1_ref, w2_ref):
    h = _norm_mod(x1, n2g_ref[...], m[3:4], m[4:5]).astype(BF16)
    hidden = w1_ref.shape[1]
    th = min(1024, hidden)
    acc = None
    for j in range(hidden // th):
        a = jnp.maximum(_dot(h, w1_ref[:, j * th:(j + 1) * th]), 0.0)
        part = _dot((a * a).astype(BF16), w2_ref[j * th:(j + 1) * th, :])
        acc = part if acc is None else acc + part
    return x1 + m[5:6] * acc


def _outffn_kernel(ret_ref, att_ref, x_ref, mod_ref, n2g_ref, wo_ref, w1_ref, w2_ref, o_ref):
    m = mod_ref[...]
    nr = ret_ref.shape[1]
    o = _dot(ret_ref[...], wo_ref[:nr, :]) + _dot(att_ref[...], wo_ref[nr:, :])
    x1 = x_ref[...] + m[2:3] * o
    o_ref[...] = _ffn_tail(x1, m, n2g_ref, w1_ref, w2_ref)


def _outffn_call(ret, att, xs, mod, layer, mod_row, n2g, w_out, w1, w2, name):
    b, l, d = xs.shape
    tm = min(ROW_TILE, l)
    row = lambda width: pl.BlockSpec((None, tm, width), lambda bi, i: (bi, i, 0))
    return pl.pallas_call(
        _outffn_kernel,
        grid=(b, l // tm),
        in_specs=[row(ret.shape[2]), row(att.shape[2]), row(d),
                  pl.BlockSpec((None, None, 6, d), lambda bi, i: (layer, mod_row(bi), 0, 0)),
                  _resident((1, d)), _resident(w_out.shape), _resident(w1.shape), _resident(w2.shape)],
        out_specs=row(d),
        out_shape=jax.ShapeDtypeStruct((b, l, d), F32),
        compiler_params=_params(2),
        name=name,
    )(ret, att, xs, mod, n2g.reshape(1, d), w_out, w1, w2)


def _odd_kernel(x_ref, mod_ref, n1g_ref, n2g_ref, win_ref, vg_ref, ws_ref, bs_ref, wout_ref,
                w1_ref, w2_ref, o_ref, uv_ref):
    m = mod_ref[...]
    x = x_ref[...]
    width = wout_ref.shape[0]
    n_chunks = x.shape[0] // CHUNK
    h = _norm_mod(x, n1g_ref[...], m[0:1], m[1:2]).astype(BF16)
    u = _gelu_tanh(_dot(h, win_ref[:, :width]))
    v = _rms(_gelu_tanh(_dot(h, win_ref[:, width:])), vg_ref[...]).astype(BF16)
    for g in range(CM_GROUPS):
        cols = slice(g * HEAD_DIM, (g + 1) * HEAD_DIM)
        vg = jnp.concatenate([v[ch * CHUNK:(ch + 1) * CHUNK, cols] for ch in range(n_chunks)], axis=1)
        mixed = _dot(ws_ref[g], vg)
        for ch in range(n_chunks):
            rows = slice(ch * CHUNK, (ch + 1) * CHUNK)
            sv = mixed[:, ch * CHUNK:(ch + 1) * CHUNK] + bs_ref[:, cols]
            uv_ref[rows, cols] = (u[rows, cols] * sv).astype(BF16)
    x1 = x + m[2:3] * _dot(uv_ref[...], wout_ref[...])
    o_ref[...] = _ffn_tail(x1, m, n2g_ref, w1_ref, w2_ref)


def _odd_call(xs, mod, layer, mod_row, n1g, n2g, w_in, v_g, w_s, b_s, w_out, w1, w2, name):
    b, l, d = xs.shape
    tm = min(ROW_TILE, l)
    width = w_out.shape[0]
    bias = jnp.repeat(b_s.T, width // CM_GROUPS, axis=1)
    row = pl.BlockSpec((None, tm, d), lambda bi, i: (bi, i, 0))
    return pl.pallas_call(
        _odd_kernel,
        grid=(b, l // tm),
        in_specs=[row, pl.BlockSpec((None, None, 6, d), lambda bi, i: (layer, mod_row(bi), 0, 0)),
                  _resident((1, d)), _resident((1, d)), _resident(w_in.shape), _resident((1, width)),
                  _resident(w_s.shape), _resident(bias.shape), _resident(w_out.shape),
                  _resident(w1.shape), _resident(w2.shape)],
        out_specs=row,
        out_shape=jax.ShapeDtypeStruct((b, l, d), F32),
        scratch_shapes=[pltpu.VMEM((tm, width), BF16)],
        compiler_params=_params(2),
        name=name,
    )(xs, mod, n1g.reshape(1, d), n2g.reshape(1, d), w_in, v_g.reshape(1, width), w_s, bias, w_out, w1, w2)


def _rope_tables(l):
    rows = l // GRID_W
    r = jnp.repeat(jnp.arange(rows, dtype=F32), GRID_W)
    c = jnp.tile(jnp.arange(GRID_W, dtype=F32), rows)
    n_freq = HEAD_DIM // 4
    inv = ROPE_BASE ** (-jnp.arange(n_freq, dtype=F32) / n_freq)
    ang = jnp.concatenate([r[:, None] * inv[None, :], c[:, None] * inv[None, :]], axis=-1)
    cos, sin = jnp.cos(ang), jnp.sin(ang)
    return jnp.concatenate([cos, cos], axis=-1), jnp.concatenate([-sin, sin], axis=-1)


def kernel(x, c, ctx, c_ctx, mod_w, mod_b, norm1_g, norm2_g, ab_w_in, ab_w_out, ret_decay, att_q_norm_g,
           att_k_norm_g, cm_w_in, cm_v_norm_g, cm_w_s, cm_b_s, cm_w_out, ff_w1, ff_w2):
    batch, seq, d = x.shape
    depth = mod_w.shape[0]
    assert seq % CHUNK == 0 and ctx.shape[1] % CHUNK == 0 and seq % GRID_W == 0
    assert ab_w_in.shape[2] == AB_HEADS * HEAD_DIM

    pad = (-(batch + 1)) % 8
    c_rows = jnp.concatenate([c, c_ctx[None, :], jnp.zeros((pad, d), F32)], axis=0)
    mod = _mod_call(c_rows, mod_w, mod_b).reshape(depth, batch + 1 + pad, 6, d)
    lat_row = lambda bi: bi
    ctx_row = lambda bi: batch

    rope_tabs = _rope_tables(seq)
    h_stream = ctx
    for l in range(depth):
        last = l == depth - 1
        i = l // 2
        w1, w2 = ff_w1[l].astype(BF16), ff_w2[l].astype(BF16)
        if l % 2 == 0:
            w_in, w_out = ab_w_in[i].astype(BF16), ab_w_out[i].astype(BF16)
            log_gamma = jax.nn.log_sigmoid(ret_decay[i].astype(F32))
            p_lat = _proj_call(x, mod, l, lat_row, norm1_g[l], w_in, att_q_norm_g[i], att_k_norm_g[i], rope_tabs)
            p_ctx = _proj_call(h_stream, mod, l, ctx_row, norm1_g[l], w_in, att_q_norm_g[i], att_k_norm_g[i], None)
            ret_l, ret_c = _ret_call(p_lat, p_ctx, log_gamma)
            att_l = _attn_call(p_lat, p_ctx, p_lat)
            x = _outffn_call(ret_l, att_l, x, mod, l, lat_row, norm2_g[l], w_out, w1, w2, "outffn_lat")
            if not last:
                att_c = _attn_call(p_ctx, p_ctx, None)
                h_stream = _outffn_call(ret_c, att_c, h_stream, mod, l, ctx_row, norm2_g[l], w_out, w1, w2,
                                        "outffn_ctx")
        else:
            cm = (cm_w_in[i].astype(BF16), cm_v_norm_g[i], cm_w_s[i].astype(BF16), cm_b_s[i],
                  cm_w_out[i].astype(BF16), w1, w2)
            x = _odd_call(x, mod, l, lat_row, norm1_g[l], norm2_g[l], *cm, "odd_lat")
            if not last:
                h_stream = _odd_call(h_stream, mod, l, ctx_row, norm1_g[l], norm2_g[l], *cm, "odd_ctx")
    return x
```

```python
import functools
import math

import jax
import jax.numpy as jnp
from jax import lax
from jax.experimental import pallas as pl
from jax.experimental.pallas import tpu as pltpu

F32 = jnp.float32
BF16 = jnp.bfloat16

EPS = 1e-6
HEAD_DIM = 128
RET_HEADS = 4
ATT_HEADS = 4
ATT_KV_HEADS = 2
ATT_GROUP = ATT_HEADS // ATT_KV_HEADS
CHUNK = 128
CM_GROUPS = 8
GRID_W = 64
ROPE_BASE = 10000.0
OFF_RQ, OFF_RK, OFF_RV, OFF_RG = 0, RET_HEADS, 2 * RET_HEADS, 3 * RET_HEADS
OFF_AQ = 4 * RET_HEADS
OFF_AK = OFF_AQ + ATT_HEADS
OFF_AV = OFF_AK + ATT_KV_HEADS
AB_HEADS = OFF_AV + ATT_KV_HEADS

V7X_VMEM_LIMIT_BYTES = 56 * 1024 * 1024
ROW_TILE = 512
ATT_Q_TILE = 1024
ATT_ROW_TILE = 256
ATT_KEY_TILE = 64
MOD_COL_TILE = 1536
LOG2E = math.log2(math.e)


def _params(n_axes):
    return pltpu.CompilerParams(dimension_semantics=("arbitrary",) * n_axes,
                                vmem_limit_bytes=V7X_VMEM_LIMIT_BYTES)


def _resident(shape):
    return pl.BlockSpec(shape, lambda *_: (0,) * len(shape), pipeline_mode=pl.Buffered(1))


def _dot(a, b):
    return jnp.dot(a, b, preferred_element_type=F32)


def _dot_nt(a, b):
    return lax.dot_general(a, b, (((1,), (1,)), ((), ())), preferred_element_type=F32)


def _dot_tn(a, b):
    return lax.dot_general(a, b, (((0,), (0,)), ((), ())), preferred_element_type=F32)


def _rms(x, g):
    return x * lax.rsqrt(jnp.mean(x * x, axis=-1, keepdims=True) + EPS) * g


def _norm_mod(x, g, shift, scale):
    return _rms(x, g) * (1.0 + scale) + shift


def _silu(x):
    return x / (1.0 + jnp.exp(-x))


def _gelu_tanh(x):
    return 0.5 * x * (1.0 + jnp.tanh(math.sqrt(2.0 / math.pi) * (x + 0.044715 * (x * x * x))))


def _mod_kernel(c_ref, w_ref, b_ref, o_ref):
    s = _silu(c_ref[...]).astype(BF16)
    o_ref[...] = _dot(s, w_ref[...].astype(BF16)) + b_ref[...]


def _mod_call(c_rows, mod_w, mod_b):
    depth, d, n = mod_w.shape
    r = c_rows.shape[0]
    tn = MOD_COL_TILE if n % MOD_COL_TILE == 0 else n
    return pl.pallas_call(
        _mod_kernel,
        grid=(depth, n // tn),
        in_specs=[pl.BlockSpec((r, d), lambda l, j: (0, 0)),
                  pl.BlockSpec((None, d, tn), lambda l, j: (l, 0, j)),
                  pl.BlockSpec((None, 1, tn), lambda l, j: (l, 0, j))],
        out_specs=pl.BlockSpec((None, r, tn), lambda l, j: (l, 0, j)),
        out_shape=jax.ShapeDtypeStruct((depth, r, n), F32),
        compiler_params=_params(2),
        name="mod",
    )(c_rows, mod_w, mod_b.reshape(depth, 1, n))


def _proj_kernel(*refs, rope):
    if rope:
        x_ref, mod_ref, g_ref, w_ref, qg_ref, kg_ref, cs_ref, sn_ref, o_ref, vt_ref = refs
        cs, sn = cs_ref[...], sn_ref[...]
    else:
        x_ref, mod_ref, g_ref, w_ref, qg_ref, kg_ref, o_ref, vt_ref = refs
    m = mod_ref[...]
    h = _norm_mod(x_ref[...], g_ref[...], m[0:1], m[1:2]).astype(BF16)

    def rot(t):
        return t * cs + pltpu.roll(t, HEAD_DIM // 2, 1) * sn if rope else t

    def store(head, t):
        o_ref[head] = t.astype(BF16)

    def segment(first, count):
        y = _dot(h, w_ref[:, first * HEAD_DIM:(first + count) * HEAD_DIM])
        return [y[:, i * HEAD_DIM:(i + 1) * HEAD_DIM] for i in range(count)]

    scale = HEAD_DIM ** -0.5
    for i, t in enumerate(segment(OFF_RQ, RET_HEADS)):
        store(OFF_RQ + i, rot(t))
    for i, t in enumerate(segment(OFF_RK, RET_HEADS)):
        store(OFF_RK + i, rot(t) * scale)
    for i, t in enumerate(segment(OFF_RV, RET_HEADS)):
        store(OFF_RV + i, t)
    for i, t in enumerate(segment(OFF_RG, RET_HEADS)):
        store(OFF_RG + i, t)
    for i, t in enumerate(segment(OFF_AQ, ATT_HEADS)):
        store(OFF_AQ + i, rot(_rms(t, qg_ref[...])) * (scale * LOG2E))
    for i, t in enumerate(segment(OFF_AK, ATT_KV_HEADS)):
        store(OFF_AK + i, rot(_rms(t, kg_ref[...])))
    for i, t in enumerate(segment(OFF_AV, ATT_KV_HEADS)):
        vt_ref[i] = t.T.astype(BF16)


def _proj_call(xs, mod, layer, mod_row, norm_g, w_in, q_g, k_g, rope_tabs):
    b, l, d = xs.shape
    n = w_in.shape[1]
    tm = min(ROW_TILE, l)
    rope = rope_tabs is not None
    in_specs = [pl.BlockSpec((None, tm, d), lambda bi, i: (bi, i, 0)),
                pl.BlockSpec((None, None, 6, d), lambda bi, i: (layer, mod_row(bi), 0, 0)),
                _resident((1, d)), _resident((d, n)),
                _resident((1, HEAD_DIM)), _resident((1, HEAD_DIM))]
    args = [xs, mod, norm_g.reshape(1, d), w_in, q_g.reshape(1, HEAD_DIM), k_g.reshape(1, HEAD_DIM)]
    if rope:
        in_specs += [pl.BlockSpec((tm, HEAD_DIM), lambda bi, i: (i, 0))] * 2
        args += list(rope_tabs)
    return pl.pallas_call(
        functools.partial(_proj_kernel, rope=rope),
        grid=(b, l // tm),
        in_specs=in_specs,
        out_specs=[pl.BlockSpec((None, OFF_AV, tm, HEAD_DIM), lambda bi, i: (bi, 0, i, 0)),
                   pl.BlockSpec((None, ATT_KV_HEADS, HEAD_DIM, tm), lambda bi, i: (bi, 0, 0, i))],
        out_shape=[jax.ShapeDtypeStruct((b, OFF_AV, l, HEAD_DIM), BF16),
                   jax.ShapeDtypeStruct((b, ATT_KV_HEADS, HEAD_DIM, l), BF16)],
        compiler_params=_params(2),
        name="proj_lat" if rope else "proj_ctx",
    )(*args)


def _ret_kernel(lg_ref, ql_ref, kl_ref, vl_ref, gl_ref, qc_ref, kc_ref, vc_ref, gc_ref,
                ol_ref, oc_ref, kv_ref, st_ref):
    c = CHUNK
    n_ctx = qc_ref.shape[0] // c
    n_lat = ql_ref.shape[0] // c
    head = pl.program_id(1)
    lgf = lg_ref[0, head]
    lgb = lg_ref[1, head]
    row = lax.broadcasted_iota(jnp.int32, (c, HEAD_DIM), 0).astype(F32)
    rel = (lax.broadcasted_iota(jnp.int32, (c, c), 0) - lax.broadcasted_iota(jnp.int32, (c, c), 1)).astype(F32)
    dmat = (jnp.where(rel >= 0, jnp.exp(jnp.maximum(rel, 0.0) * lgf), 0.0)
            + jnp.where(rel <= 0, jnp.exp(jnp.maximum(-rel, 0.0) * lgb), 0.0))
    kdf = jnp.exp((c - 1.0 - row) * lgf)
    kdb = jnp.exp(row * lgb)
    qdf = jnp.exp((row + 1.0) * lgf)
    qdb = jnp.exp((c - row) * lgb)
    cdf = jnp.exp(jnp.full((HEAD_DIM, HEAD_DIM), c, F32) * lgf)
    cdb = jnp.exp(jnp.full((HEAD_DIM, HEAD_DIM), c, F32) * lgb)

    chunks = ([(qc_ref, kc_ref, vc_ref, gc_ref, oc_ref, n) for n in range(n_ctx)]
              + [(ql_ref, kl_ref, vl_ref, gl_ref, ol_ref, n) for n in range(n_lat)])
    n_all = len(chunks)

    for t, (_, k_ref, v_ref, _, _, n) in enumerate(chunks):
        rows = slice(n * c, (n + 1) * c)
        v = v_ref[rows, :].astype(F32)
        vd = jnp.concatenate([(v * kdf).astype(BF16), (v * kdb).astype(BF16)], axis=1)
        kv_ref[t] = _dot_tn(k_ref[rows, :], vd)

    state = jnp.zeros((HEAD_DIM, HEAD_DIM), F32)
    for t in range(n_all):
        st_ref[t, :HEAD_DIM, :] = state.astype(BF16)
        state = cdf * state + kv_ref[t, :, :HEAD_DIM]
    state = jnp.zeros((HEAD_DIM, HEAD_DIM), F32)
    for t in list(reversed(range(n_ctx))) + list(reversed(range(n_ctx, n_all))):
        st_ref[t, HEAD_DIM:, :] = state.astype(BF16)
        state = cdb * state + kv_ref[t, :, HEAD_DIM:]

    for t, (q_ref, k_ref, v_ref, g_ref, o_ref, n) in enumerate(chunks):
        rows = slice(n * c, (n + 1) * c)
        q = q_ref[rows, :]
        qf = q.astype(F32)
        a = (_dot_nt(q, k_ref[rows, :]) * dmat).astype(BF16)
        lhs = jnp.concatenate([a, (qf * qdf).astype(BF16), (qf * qdb).astype(BF16)], axis=1)
        o = _dot(lhs, jnp.concatenate([v_ref[rows, :], st_ref[t]], axis=0))
        mu = jnp.mean(o, axis=-1, keepdims=True)
        dev = o - mu
        y = dev * lax.rsqrt(jnp.mean(dev * dev, axis=-1, keepdims=True) + EPS)
        o_ref[rows, :] = (y * _silu(g_ref[rows, :].astype(F32))).astype(BF16)


def _ret_call(p_lat, p_ctx, log_gamma):
    b, _, l, _ = p_lat.shape
    lc = p_ctx.shape[2]
    n_all = (l + lc) // CHUNK

    def head_spec(rows, off):
        return pl.BlockSpec((None, None, rows, HEAD_DIM), lambda bi, h: (bi, off + h, 0, 0))

    return pl.pallas_call(
        _ret_kernel,
        grid=(b, RET_HEADS),
        in_specs=[pl.BlockSpec(memory_space=pltpu.SMEM)]
        + [head_spec(l, off) for off in (OFF_RQ, OFF_RK, OFF_RV, OFF_RG)]
        + [head_spec(lc, off) for off in (OFF_RQ, OFF_RK, OFF_RV, OFF_RG)],
        out_specs=[head_spec(l, 0), head_spec(lc, 0)],
        out_shape=[jax.ShapeDtypeStruct((b, RET_HEADS, l, HEAD_DIM), BF16),
                   jax.ShapeDtypeStruct((b, RET_HEADS, lc, HEAD_DIM), BF16)],
        scratch_shapes=[pltpu.VMEM((n_all, HEAD_DIM, 2 * HEAD_DIM), F32),
                        pltpu.VMEM((n_all, 2 * HEAD_DIM, HEAD_DIM), BF16)],
        compiler_params=_params(2),
        name="retention",
    )(log_gamma, p_lat, p_lat, p_lat, p_lat, p_ctx, p_ctx, p_ctx, p_ctx)


def _fold_rows(x, rows, op):
    parts = [x[i:i + rows, :] for i in range(0, x.shape[0], rows)]
    while len(parts) > 1:
        parts = [op(parts[i], parts[i + 1]) for i in range(0, len(parts) - 1, 2)] + parts[len(parts) & ~1:]
    return parts[0]


def _attn_kernel(*refs, with_lat):
    if with_lat:
        q_ref, kc_ref, vtc_ref, kl_ref, vtl_ref, o_ref, *scratch = refs
    else:
        q_ref, kc_ref, vtc_ref, o_ref, *scratch = refs
    tq = q_ref.shape[1]
    lc = kc_ref.shape[0]
    n_keys = scratch[0].shape[0]
    rt = scratch[0].shape[1] // ATT_GROUP
    kt = ATT_KEY_TILE
    sub = 8

    for u in range(tq // rt):
        s_ref, p_ref = scratch[2 * (u % 2)], scratch[2 * (u % 2) + 1]
        q_rows = slice(u * rt, (u + 1) * rt)
        q2 = jnp.concatenate([q_ref[g, q_rows, :] for g in range(ATT_GROUP)], axis=0)
        s_ref[:lc, :] = _dot_nt(kc_ref[...], q2)
        if with_lat:
            s_ref[lc:, :] = _dot_nt(kl_ref[...], q2)
        m = None
        for start in range(0, n_keys, kt):
            t = _fold_rows(s_ref[start:start + kt, :], sub, jnp.maximum)
            m = t if m is None else jnp.maximum(m, t)
        m = jnp.max(m, axis=0, keepdims=True)
        den = None
        for start in range(0, n_keys, kt):
            p = jnp.exp2(s_ref[start:start + kt, :] - m)
            t = _fold_rows(p, sub, jnp.add)
            den = t if den is None else den + t
            p_ref[start:start + kt, :] = p.astype(BF16)
        acc = _dot(vtc_ref[...], p_ref[:lc, :])
        if with_lat:
            acc = acc + _dot(vtl_ref[...], p_ref[lc:, :])
        out = acc / jnp.sum(den, axis=0, keepdims=True)
        for g in range(ATT_GROUP):
            o_ref[g, q_rows, :] = out[:, g * rt:(g + 1) * rt].T.astype(BF16)


def _attn_call(p_q, ctx_kv, lat_kv):
    b, _, lq, _ = p_q.shape
    lc = ctx_kv[0].shape[2]
    tq = min(ATT_Q_TILE, lq)
    rt = min(ATT_ROW_TILE, tq)
    with_lat = lat_kv is not None
    n_keys = lc + (lat_kv[0].shape[2] if with_lat else 0)

    def k_spec(rows):
        return pl.BlockSpec((None, None, rows, HEAD_DIM), lambda bi, kv, i: (bi, OFF_AK + kv, 0, 0))

    def vt_spec(rows):
        return pl.BlockSpec((None, None, HEAD_DIM, rows), lambda bi, kv, i: (bi, kv, 0, 0))

    in_specs = [pl.BlockSpec((None, ATT_GROUP, tq, HEAD_DIM), lambda bi, kv, i: (bi, OFF_AQ // ATT_GROUP + kv, i, 0)),
                k_spec(lc), vt_spec(lc)]
    args = [p_q, *ctx_kv]
    if with_lat:
        in_specs += [k_spec(n_keys - lc), vt_spec(n_keys - lc)]
        args += list(lat_kv)
    return pl.pallas_call(
        functools.partial(_attn_kernel, with_lat=with_lat),
        grid=(b, ATT_KV_HEADS, lq // tq),
        in_specs=in_specs,
        out_specs=pl.BlockSpec((None, ATT_GROUP, tq, HEAD_DIM), lambda bi, kv, i: (bi, kv, i, 0)),
        out_shape=jax.ShapeDtypeStruct((b, ATT_HEADS, lq, HEAD_DIM), BF16),
        scratch_shapes=[pltpu.VMEM((n_keys, ATT_GROUP * rt), F32),
                        pltpu.VMEM((n_keys, ATT_GROUP * rt), BF16)] * 2,
        compiler_params=_params(3),
        name="attn_lat" if with_lat else "attn_ctx",
    )(*args)


def _ffn_tail(x1, m, n2g_ref, w1_ref, w2_ref):
    h = _norm_mod(x1, n2g_ref[...], m[3:4], m[4:5]).astype(BF16)
    hidden = w1_ref.shape[1]
    th = min(1024, hidden)
    acc = None
    for j in range(hidden // th):
        a = jnp.maximum(_dot(h, w1_ref[:, j * th:(j + 1) * th]), 0.0)
        part = _dot((a * a).astype(BF16), w2_ref[j * th:(j + 1) * th, :])
        acc = part if acc is None else acc + part
    return x1 + m[5:6] * acc


def _outffn_kernel(ret_ref, att_ref, x_ref, mod_ref, n2g_ref, wo_ref, w1_ref, w2_ref, o_ref):
    m = mod_ref[...]
    mixed = jnp.concatenate([ret_ref[h] for h in range(ret_ref.shape[0])]
                            + [att_ref[h] for h in range(att_ref.shape[0])], axis=1)
    x1 = x_ref[...] + m[2:3] * _dot(mixed, wo_ref[...])
    o_ref[...] = _ffn_tail(x1, m, n2g_ref, w1_ref, w2_ref)


def _outffn_call(ret, att, xs, mod, layer, mod_row, n2g, w_out, w1, w2, name):
    b, l, d = xs.shape
    tm = min(ROW_TILE, l)
    row = pl.BlockSpec((None, tm, d), lambda bi, i: (bi, i, 0))
    heads = lambda a: pl.BlockSpec((None, a.shape[1], tm, HEAD_DIM), lambda bi, i: (bi, 0, i, 0))
    return pl.pallas_call(
        _outffn_kernel,
        grid=(b, l // tm),
        in_specs=[heads(ret), heads(att), row,
                  pl.BlockSpec((None, None, 6, d), lambda bi, i: (layer, mod_row(bi), 0, 0)),
                  _resident((1, d)), _resident(w_out.shape), _resident(w1.shape), _resident(w2.shape)],
        out_specs=row,
        out_shape=jax.ShapeDtypeStruct((b, l, d), F32),
        compiler_params=_params(2),
        name=name,
    )(ret, att, xs, mod, n2g.reshape(1, d), w_out, w1, w2)


def _odd_kernel(x_ref, mod_ref, n1g_ref, n2g_ref, win_ref, vg_ref, ws_ref, bs_ref, wout_ref,
                w1_ref, w2_ref, o_ref, uv_ref):
    m = mod_ref[...]
    x = x_ref[...]
    width = wout_ref.shape[0]
    n_chunks = x.shape[0] // CHUNK
    h = _norm_mod(x, n1g_ref[...], m[0:1], m[1:2]).astype(BF16)
    u = _gelu_tanh(_dot(h, win_ref[:, :width]))
    v = _rms(_gelu_tanh(_dot(h, win_ref[:, width:])), vg_ref[...]).astype(BF16)
    for g in range(CM_GROUPS):
        cols = slice(g * HEAD_DIM, (g + 1) * HEAD_DIM)
        vg = jnp.concatenate([v[ch * CHUNK:(ch + 1) * CHUNK, cols] for ch in range(n_chunks)], axis=1)
        mixed = _dot(ws_ref[g], vg)
        for ch in range(n_chunks):
            rows = slice(ch * CHUNK, (ch + 1) * CHUNK)
            sv = mixed[:, ch * CHUNK:(ch + 1) * CHUNK] + bs_ref[:, cols]
            uv_ref[rows, cols] = (u[rows, cols] * sv).astype(BF16)
    x1 = x + m[2:3] * _dot(uv_ref[...], wout_ref[...])
    o_ref[...] = _ffn_tail(x1, m, n2g_ref, w1_ref, w2_ref)


def _odd_call(xs, mod, layer, mod_row, n1g, n2g, w_in, v_g, w_s, b_s, w_out, w1, w2, name):
    b, l, d = xs.shape
    tm = min(ROW_TILE, l)
    width = w_out.shape[0]
    bias = jnp.repeat(b_s.T, width // CM_GROUPS, axis=1)
    row = pl.BlockSpec((None, tm, d), lambda bi, i: (bi, i, 0))
    return pl.pallas_call(
        _odd_kernel,
        grid=(b, l // tm),
        in_specs=[row, pl.BlockSpec((None, None, 6, d), lambda bi, i: (layer, mod_row(bi), 0, 0)),
                  _resident((1, d)), _resident((1, d)), _resident(w_in.shape), _resident((1, width)),
                  _resident(w_s.shape), _resident(bias.shape), _resident(w_out.shape),
                  _resident(w1.shape), _resident(w2.shape)],
        out_specs=row,
        out_shape=jax.ShapeDtypeStruct((b, l, d), F32),
        scratch_shapes=[pltpu.VMEM((tm, width), BF16)],
        compiler_params=_params(2),
        name=name,
    )(xs, mod, n1g.reshape(1, d), n2g.reshape(1, d), w_in, v_g.reshape(1, width), w_s, bias, w_out, w1, w2)


def _rope_tables(l):
    rows = l // GRID_W
    r = jnp.repeat(jnp.arange(rows, dtype=F32), GRID_W)
    c = jnp.tile(jnp.arange(GRID_W, dtype=F32), rows)
    n_freq = HEAD_DIM // 4
    inv = ROPE_BASE ** (-jnp.arange(n_freq, dtype=F32) / n_freq)
    ang = jnp.concatenate([r[:, None] * inv[None, :], c[:, None] * inv[None, :]], axis=-1)
    cos, sin = jnp.cos(ang), jnp.sin(ang)
    return jnp.concatenate([cos, cos], axis=-1), jnp.concatenate([-sin, sin], axis=-1)


def kernel(x, c, ctx, c_ctx, mod_w, mod_b, norm1_g, norm2_g, ab_w_in, ab_w_out, ret_decay, att_q_norm_g,
           att_k_norm_g, cm_w_in, cm_v_norm_g, cm_w_s, cm_b_s, cm_w_out, ff_w1, ff_w2):
    batch, seq, d = x.shape
    depth = mod_w.shape[0]
    assert seq % CHUNK == 0 and ctx.shape[1] % CHUNK == 0 and seq % GRID_W == 0
    assert ab_w_in.shape[2] == AB_HEADS * HEAD_DIM

    pad = (-(batch + 1)) % 8
    c_rows = jnp.concatenate([c, c_ctx[None, :], jnp.zeros((pad, d), F32)], axis=0)
    mod = _mod_call(c_rows, mod_w, mod_b).reshape(depth, batch + 1 + pad, 6, d)
    lat_row = lambda bi: bi
    ctx_row = lambda bi: batch

    rope_tabs = _rope_tables(seq)
    h_stream = ctx
    for l in range(depth):
        last = l == depth - 1
        i = l // 2
        w1, w2 = ff_w1[l].astype(BF16), ff_w2[l].astype(BF16)
        if l % 2 == 0:
            w_in, w_out = ab_w_in[i].astype(BF16), ab_w_out[i].astype(BF16)
            log_gamma = jax.nn.log_sigmoid(ret_decay[i].astype(F32))
            p_lat = _proj_call(x, mod, l, lat_row, norm1_g[l], w_in, att_q_norm_g[i], att_k_norm_g[i], rope_tabs)
            p_ctx = _proj_call(h_stream, mod, l, ctx_row, norm1_g[l], w_in, att_q_norm_g[i], att_k_norm_g[i], None)
            ret_l, ret_c = _ret_call(p_lat[0], p_ctx[0], log_gamma)
            att_l = _attn_call(p_lat[0], p_ctx, p_lat)
            x = _outffn_call(ret_l, att_l, x, mod, l, lat_row, norm2_g[l], w_out, w1, w2, "outffn_lat")
            if not last:
                att_c = _attn_call(p_ctx[0], p_ctx, None)
                h_stream = _outffn_call(ret_c, att_c, h_stream, mod, l, ctx_row, norm2_g[l], w_out, w1, w2,
                                        "outffn_ctx")
        else:
            cm = (cm_w_in[i].astype(BF16), cm_v_norm_g[i], cm_w_s[i].astype(BF16), cm_b_s[i],
                  cm_w_out[i].astype(BF16), w1, w2)
            x = _odd_call(x, mod, l, lat_row, norm1_g[l], norm2_g[l], *cm, "odd_lat")
            if not last:
                h_stream = _odd_call(h_stream, mod, l, ctx_row, norm1_g[l], norm2_g[l], *cm, "odd_ctx")
    return x
```

```python
import functools
import math

import jax
import jax.numpy as jnp
from jax import lax
from jax.experimental import pallas as pl
from jax.experimental.pallas import tpu as pltpu

F32 = jnp.float32
BF16 = jnp.bfloat16

EPS = 1e-6
HEAD_DIM = 128
RET_HEADS = 4
ATT_HEADS = 4
ATT_KV_HEADS = 2
ATT_GROUP = ATT_HEADS // ATT_KV_HEADS
CHUNK = 128
CM_GROUPS = 8
GRID_W = 64
ROPE_BASE = 10000.0
OFF_RQ, OFF_RK, OFF_RV, OFF_RG = 0, RET_HEADS, 2 * RET_HEADS, 3 * RET_HEADS
OFF_AQ = 4 * RET_HEADS
OFF_AK = OFF_AQ + ATT_HEADS
OFF_AV = OFF_AK + ATT_KV_HEADS
AB_HEADS = OFF_AV + ATT_KV_HEADS

V7X_VMEM_LIMIT_BYTES = 56 * 1024 * 1024
ROW_TILE = 512
ATT_Q_TILE = 1024
ATT_ROW_TILE = 256
ATT_KEY_TILE = 16
MOD_COL_TILE = 1536
LOG2E = math.log2(math.e)


def _params(n_axes):
    return pltpu.CompilerParams(dimension_semantics=("arbitrary",) * n_axes,
                                vmem_limit_bytes=V7X_VMEM_LIMIT_BYTES)


def _resident(shape):
    return pl.BlockSpec(shape, lambda *_: (0,) * len(shape), pipeline_mode=pl.Buffered(1))


def _dot(a, b):
    return jnp.dot(a, b, preferred_element_type=F32)


def _dot_nt(a, b):
    return lax.dot_general(a, b, (((1,), (1,)), ((), ())), preferred_element_type=F32)


def _dot_tn(a, b):
    return lax.dot_general(a, b, (((0,), (0,)), ((), ())), preferred_element_type=F32)


def _rms(x, g):
    return x * lax.rsqrt(jnp.mean(x * x, axis=-1, keepdims=True) + EPS) * g


def _norm_affine(x, gain, shift):
    return x * lax.rsqrt(jnp.mean(x * x, axis=-1, keepdims=True) + EPS) * gain + shift


def _silu(x):
    return x / (1.0 + jnp.exp(-x))


def _gelu_tanh(x):
    return 0.5 * x * (1.0 + jnp.tanh(math.sqrt(2.0 / math.pi) * (x + 0.044715 * (x * x * x))))


def _mod_kernel(c_ref, w_ref, b_ref, o_ref):
    s = _silu(c_ref[...]).astype(BF16)
    o_ref[...] = _dot(s, w_ref[...].astype(BF16)) + b_ref[...]


def _mod_call(c_rows, mod_w, mod_b):
    depth, d, n = mod_w.shape
    r = c_rows.shape[0]
    tn = MOD_COL_TILE if n % MOD_COL_TILE == 0 else n
    return pl.pallas_call(
        _mod_kernel,
        grid=(depth, n // tn),
        in_specs=[pl.BlockSpec((r, d), lambda l, j: (0, 0)),
                  pl.BlockSpec((None, d, tn), lambda l, j: (l, 0, j)),
                  pl.BlockSpec((None, 1, tn), lambda l, j: (l, 0, j))],
        out_specs=pl.BlockSpec((None, r, tn), lambda l, j: (l, 0, j)),
        out_shape=jax.ShapeDtypeStruct((depth, r, n), F32),
        compiler_params=_params(2),
        name="mod",
    )(c_rows, mod_w, mod_b.reshape(depth, 1, n))


def _proj_kernel(*refs, rope):
    if rope:
        x_ref, mod_ref, g_ref, w_ref, qg_ref, kg_ref, cs_ref, sn_ref, o_ref, vt_ref = refs
    else:
        x_ref, mod_ref, g_ref, w_ref, qg_ref, kg_ref, o_ref, vt_ref = refs
    m = mod_ref[...]
    gm = g_ref[...] * (1.0 + m[1:2])
    scale = HEAD_DIM ** -0.5
    qg = qg_ref[...] * (scale * LOG2E)
    kg = kg_ref[...]
    tm = x_ref.shape[0]
    sub = tm // 2 if tm % (2 * CHUNK) == 0 else tm

    for r0 in range(0, tm, sub):
        rows = slice(r0, r0 + sub)
        h = _norm_affine(x_ref[rows, :], gm, m[0:1]).astype(BF16)
        if rope:
            cs, sn = cs_ref[rows, :], sn_ref[rows, :]
            cs_k, sn_k = cs * scale, sn * scale

        def rot(t, scaled=False):
            if not rope:
                return t * scale if scaled else t
            c, s = (cs_k, sn_k) if scaled else (cs, sn)
            return t * c + pltpu.roll(t, HEAD_DIM // 2, 1) * s

        def segment(first, count):
            y = _dot(h, w_ref[:, first * HEAD_DIM:(first + count) * HEAD_DIM])
            return [y[:, i * HEAD_DIM:(i + 1) * HEAD_DIM] for i in range(count)]

        for i, t in enumerate(segment(OFF_AQ, ATT_HEADS)):
            o_ref[OFF_AQ + i, rows, :] = rot(_rms(t, qg)).astype(BF16)
        for i, t in enumerate(segment(OFF_AK, ATT_KV_HEADS)):
            o_ref[OFF_AK + i, rows, :] = rot(_rms(t, kg)).astype(BF16)
        for i, t in enumerate(segment(OFF_AV, ATT_KV_HEADS)):
            vt_ref[i, :, rows] = t.T.astype(BF16)
        for i, t in enumerate(segment(OFF_RQ, RET_HEADS)):
            o_ref[OFF_RQ + i, rows, :] = rot(t).astype(BF16)
        for i, t in enumerate(segment(OFF_RK, RET_HEADS)):
            o_ref[OFF_RK + i, rows, :] = rot(t, scaled=True).astype(BF16)
        for i, t in enumerate(segment(OFF_RV, RET_HEADS)):
            o_ref[OFF_RV + i, rows, :] = t.astype(BF16)
        for i, t in enumerate(segment(OFF_RG, RET_HEADS)):
            o_ref[OFF_RG + i, rows, :] = t.astype(BF16)


def _proj_call(xs, mod, layer, mod_row, norm_g, w_in, q_g, k_g, rope_tabs):
    b, l, d = xs.shape
    n = w_in.shape[1]
    tm = min(ROW_TILE, l)
    rope = rope_tabs is not None
    in_specs = [pl.BlockSpec((None, tm, d), lambda bi, i: (bi, i, 0)),
                pl.BlockSpec((None, None, 6, d), lambda bi, i: (layer, mod_row(bi), 0, 0)),
                _resident((1, d)), _resident((d, n)),
                _resident((1, HEAD_DIM)), _resident((1, HEAD_DIM))]
    args = [xs, mod, norm_g.reshape(1, d), w_in, q_g.reshape(1, HEAD_DIM), k_g.reshape(1, HEAD_DIM)]
    if rope:
        in_specs += [pl.BlockSpec((tm, HEAD_DIM), lambda bi, i: (i, 0))] * 2
        args += list(rope_tabs)
    return pl.pallas_call(
        functools.partial(_proj_kernel, rope=rope),
        grid=(b, l // tm),
        in_specs=in_specs,
        out_specs=[pl.BlockSpec((None, OFF_AV, tm, HEAD_DIM), lambda bi, i: (bi, 0, i, 0)),
                   pl.BlockSpec((None, ATT_KV_HEADS, HEAD_DIM, tm), lambda bi, i: (bi, 0, 0, i))],
        out_shape=[jax.ShapeDtypeStruct((b, OFF_AV, l, HEAD_DIM), BF16),
                   jax.ShapeDtypeStruct((b, ATT_KV_HEADS, HEAD_DIM, l), BF16)],
        compiler_params=_params(2),
        name="proj_lat" if rope else "proj_ctx",
    )(*args)


def _ret_kernel(lg_ref, ql_ref, kl_ref, vl_ref, gl_ref, qc_ref, kc_ref, vc_ref, gc_ref,
                ol_ref, oc_ref, kv_ref, st_ref, lhs_ref):
    c = CHUNK
    n_ctx = qc_ref.shape[0] // c
    n_lat = ql_ref.shape[0] // c
    head = pl.program_id(1)
    lgf = lg_ref[0, head]
    lgb = lg_ref[1, head]
    row = lax.broadcasted_iota(jnp.int32, (c, HEAD_DIM), 0).astype(F32)
    rel = (lax.broadcasted_iota(jnp.int32, (c, c), 0) - lax.broadcasted_iota(jnp.int32, (c, c), 1)).astype(F32)
    dmat = (jnp.where(rel >= 0, jnp.exp(jnp.maximum(rel, 0.0) * lgf), 0.0)
            + jnp.where(rel <= 0, jnp.exp(jnp.maximum(-rel, 0.0) * lgb), 0.0))
    kdf = jnp.exp((c - 1.0 - row) * lgf)
    kdb = jnp.exp(row * lgb)
    qdf = jnp.exp((row + 1.0) * lgf)
    qdb = jnp.exp((c - row) * lgb)
    cdf = jnp.exp(jnp.full((HEAD_DIM, HEAD_DIM), c, F32) * lgf)
    cdb = jnp.exp(jnp.full((HEAD_DIM, HEAD_DIM), c, F32) * lgb)

    chunks = ([(qc_ref, kc_ref, vc_ref, gc_ref, oc_ref, n) for n in range(n_ctx)]
              + [(ql_ref, kl_ref, vl_ref, gl_ref, ol_ref, n) for n in range(n_lat)])
    n_all = len(chunks)

    for t, (q_ref, k_ref, v_ref, _, _, n) in enumerate(chunks):
        rows = slice(n * c, (n + 1) * c)
        k = k_ref[rows, :]
        v = v_ref[rows, :].astype(F32)
        vd = jnp.concatenate([(v * kdf).astype(BF16), (v * kdb).astype(BF16)], axis=1)
        kv_ref[t] = _dot_tn(k, vd)
        q = q_ref[rows, :]
        qf = q.astype(F32)
        a = (_dot_nt(q, k) * dmat).astype(BF16)
        lhs_ref[t] = jnp.concatenate([a, (qf * qdf).astype(BF16), (qf * qdb).astype(BF16)], axis=1)

    state = jnp.zeros((HEAD_DIM, HEAD_DIM), F32)
    for t in range(n_all):
        st_ref[t, :HEAD_DIM, :] = state.astype(BF16)
        state = cdf * state + kv_ref[t, :, :HEAD_DIM]
    state = jnp.zeros((HEAD_DIM, HEAD_DIM), F32)
    for t in list(reversed(range(n_ctx))) + list(reversed(range(n_ctx, n_all))):
        st_ref[t, HEAD_DIM:, :] = state.astype(BF16)
        state = cdb * state + kv_ref[t, :, HEAD_DIM:]

    for t, (_, _, v_ref, g_ref, o_ref, n) in enumerate(chunks):
        rows = slice(n * c, (n + 1) * c)
        o = _dot(lhs_ref[t], jnp.concatenate([v_ref[rows, :], st_ref[t]], axis=0))
        mu = jnp.mean(o, axis=-1, keepdims=True)
        dev = o - mu
        y = dev * lax.rsqrt(jnp.mean(dev * dev, axis=-1, keepdims=True) + EPS)
        o_ref[rows, :] = (y * _silu(g_ref[rows, :].astype(F32))).astype(BF16)


def _ret_call(p_lat, p_ctx, log_gamma):
    b, _, l, _ = p_lat.shape
    lc = p_ctx.shape[2]
    n_all = (l + lc) // CHUNK

    def head_spec(rows, off):
        return pl.BlockSpec((None, None, rows, HEAD_DIM), lambda bi, h: (bi, off + h, 0, 0))

    return pl.pallas_call(
        _ret_kernel,
        grid=(b, RET_HEADS),
        in_specs=[pl.BlockSpec(memory_space=pltpu.SMEM)]
        + [head_spec(l, off) for off in (OFF_RQ, OFF_RK, OFF_RV, OFF_RG)]
        + [head_spec(lc, off) for off in (OFF_RQ, OFF_RK, OFF_RV, OFF_RG)],
        out_specs=[head_spec(l, 0), head_spec(lc, 0)],
        out_shape=[jax.ShapeDtypeStruct((b, RET_HEADS, l, HEAD_DIM), BF16),
                   jax.ShapeDtypeStruct((b, RET_HEADS, lc, HEAD_DIM), BF16)],
        scratch_shapes=[pltpu.VMEM((n_all, HEAD_DIM, 2 * HEAD_DIM), F32),
                        pltpu.VMEM((n_all, 2 * HEAD_DIM, HEAD_DIM), BF16),
                        pltpu.VMEM((n_all, CHUNK, CHUNK + 2 * HEAD_DIM), BF16)],
        compiler_params=_params(2),
        name="retention",
    )(log_gamma, p_lat, p_lat, p_lat, p_lat, p_ctx, p_ctx, p_ctx, p_ctx)


def _fold_rows(x, rows, op):
    parts = [x[i:i + rows, :] for i in range(0, x.shape[0], rows)]
    while len(parts) > 1:
        parts = [op(parts[i], parts[i + 1]) for i in range(0, len(parts) - 1, 2)] + parts[len(parts) & ~1:]
    return parts[0]


def _attn_kernel(*refs, with_lat):
    if with_lat:
        q_ref, kc_ref, vtc_ref, kl_ref, vtl_ref, o_ref, s_ref, p_ref = refs
    else:
        q_ref, kc_ref, vtc_ref, o_ref, s_ref, p_ref = refs
        kl_ref = vtl_ref = None
    tq = q_ref.shape[1]
    lc = kc_ref.shape[0]
    n_keys = s_ref.shape[1]
    rt = s_ref.shape[2] // ATT_GROUP
    kt = ATT_KEY_TILE
    sub = 8

    n_units = tq // rt
    piece = 4 * HEAD_DIM
    pieces = [(0, lc)] + [(s, min(s + piece, n_keys)) for s in range(lc, n_keys, piece)]

    def keys(ref_c, ref_l, lo, hi, axis):
        ref, lo, hi = (ref_c, lo, hi) if lo < lc else (ref_l, lo - lc, hi - lc)
        return ref[lo:hi, :] if axis == 0 else ref[:, lo:hi]

    def unit_rows(u):
        return pl.ds(u * rt, rt) if isinstance(u, int) else pl.ds(pl.multiple_of(u * rt, rt), rt)

    def scores(u, slot):
        q2 = jnp.concatenate([q_ref[g, unit_rows(u), :] for g in range(ATT_GROUP)], axis=0)
        for lo, hi in pieces:
            s_ref[slot, lo:hi, :] = _dot_nt(keys(kc_ref, kl_ref, lo, hi, 0), q2)

    def softmax(slot):
        m = None
        for start in range(0, n_keys, kt):
            t = _fold_rows(s_ref[slot, start:start + kt, :], sub, jnp.maximum)
            m = t if m is None else jnp.maximum(m, t)
        m = jnp.max(m, axis=0, keepdims=True)
        den = None
        for start in range(0, n_keys, kt):
            p = jnp.exp2(s_ref[slot, start:start + kt, :] - m)
            t = _fold_rows(p, sub, jnp.add)
            den = t if den is None else den + t
            p_ref[slot, start:start + kt, :] = p.astype(BF16)
        return den

    def pv(u, slot, den):
        acc = None
        for lo, hi in pieces:
            part = _dot(keys(vtc_ref, vtl_ref, lo, hi, 1), p_ref[slot, lo:hi, :])
            acc = part if acc is None else acc + part
        out = acc / jnp.sum(den, axis=0, keepdims=True)
        for g in range(ATT_GROUP):
            o_ref[g, unit_rows(u), :] = out[:, g * rt:(g + 1) * rt].T.astype(BF16)

    def steady(u, den_prev):
        slot = u % 2
        scores(u + 1, 1 - slot)
        den_u = softmax(slot)
        pv(u - 1, 1 - slot, den_prev)
        return den_u

    scores(0, 0)
    if n_units == 1:
        pv(0, 0, softmax(0))
    else:
        scores(1, 1)
        den = softmax(0)
        for u in range(1, n_units - 1):
            den = steady(u, den)
        last = n_units - 1
        den_last = softmax(last % 2)
        pv(last - 1, (last - 1) % 2, den)
        pv(last, last % 2, den_last)


def _attn_call(p_q, ctx_kv, lat_kv):
    b, _, lq, _ = p_q.shape
    lc = ctx_kv[0].shape[2]
    tq = min(ATT_Q_TILE, lq)
    rt = min(ATT_ROW_TILE, tq)
    with_lat = lat_kv is not None
    n_keys = lc + (lat_kv[0].shape[2] if with_lat else 0)

    def k_spec(rows):
        return pl.BlockSpec((None, None, rows, HEAD_DIM), lambda bi, kv, i: (bi, OFF_AK + kv, 0, 0))

    def vt_spec(rows):
        return pl.BlockSpec((None, None, HEAD_DIM, rows), lambda bi, kv, i: (bi, kv, 0, 0))

    in_specs = [pl.BlockSpec((None, ATT_GROUP, tq, HEAD_DIM), lambda bi, kv, i: (bi, OFF_AQ // ATT_GROUP + kv, i, 0)),
                k_spec(lc), vt_spec(lc)]
    args = [p_q, *ctx_kv]
    if with_lat:
        in_specs += [k_spec(n_keys - lc), vt_spec(n_keys - lc)]
        args += list(lat_kv)
    return pl.pallas_call(
        functools.partial(_attn_kernel, with_lat=with_lat),
        grid=(b, ATT_KV_HEADS, lq // tq),
        in_specs=in_specs,
        out_specs=pl.BlockSpec((None, ATT_GROUP, tq, HEAD_DIM), lambda bi, kv, i: (bi, kv, i, 0)),
        out_shape=jax.ShapeDtypeStruct((b, ATT_HEADS, lq, HEAD_DIM), BF16),
        scratch_shapes=[pltpu.VMEM((2, n_keys, ATT_GROUP * rt), F32),
                        pltpu.VMEM((2, n_keys, ATT_GROUP * rt), BF16)],
        compiler_params=_params(3),
        name="attn_lat" if with_lat else "attn_ctx",
    )(*args)


def _ffn_tail(x1, m, n2g_ref, w1_ref, w2_ref):
    h = _norm_affine(x1, n2g_ref[...] * (1.0 + m[4:5]), m[3:4]).astype(BF16)
    hidden = w1_ref.shape[1]
    th = min(1024, hidden)
    acc = None
    for j in range(hidden // th):
        a = jnp.maximum(_dot(h, w1_ref[:, j * th:(j + 1) * th]), 0.0)
        part = _dot((a * a).astype(BF16), w2_ref[j * th:(j + 1) * th, :])
        acc = part if acc is None else acc + part
    return x1 + m[5:6] * acc


def _outffn_kernel(ret_ref, att_ref, x_ref, mod_ref, n2g_ref, wo_ref, w1_ref, w2_ref, o_ref):
    m = mod_ref[...]
    mixed = jnp.concatenate([ret_ref[h] for h in range(ret_ref.shape[0])]
                            + [att_ref[h] for h in range(att_ref.shape[0])], axis=1)
    x1 = x_ref[...] + m[2:3] * _dot(mixed, wo_ref[...])
    o_ref[...] = _ffn_tail(x1, m, n2g_ref, w1_ref, w2_ref)


def _outffn_call(ret, att, xs, mod, layer, mod_row, n2g, w_out, w1, w2, name):
    b, l, d = xs.shape
    tm = min(ROW_TILE, l)
    row = pl.BlockSpec((None, tm, d), lambda bi, i: (bi, i, 0))
    heads = lambda a: pl.BlockSpec((None, a.shape[1], tm, HEAD_DIM), lambda bi, i: (bi, 0, i, 0))
    return pl.pallas_call(
        _outffn_kernel,
        grid=(b, l // tm),
        in_specs=[heads(ret), heads(att), row,
                  pl.BlockSpec((None, None, 6, d), lambda bi, i: (layer, mod_row(bi), 0, 0)),
                  _resident((1, d)), _resident(w_out.shape), _resident(w1.shape), _resident(w2.shape)],
        out_specs=row,
        out_shape=jax.ShapeDtypeStruct((b, l, d), F32),
        compiler_params=_params(2),
        name=name,
    )(ret, att, xs, mod, n2g.reshape(1, d), w_out, w1, w2)


def _odd_kernel(x_ref, mod_ref, n1g_ref, n2g_ref, win_ref, vg_ref, ws_ref, bs_ref, wout_ref,
                w1_ref, w2_ref, o_ref, uv_ref):
    m = mod_ref[...]
    tm = x_ref.shape[0]
    width = wout_ref.shape[0]
    gain = n1g_ref[...] * (1.0 + m[1:2])
    sub = 2 * CHUNK if tm % (2 * CHUNK) == 0 else CHUNK
    n_chunks = sub // CHUNK
    for r0 in range(0, tm, sub):
        h = _norm_affine(x_ref[r0:r0 + sub, :], gain, m[0:1]).astype(BF16)
        u = _gelu_tanh(_dot(h, win_ref[:, :width]))
        v = _rms(_gelu_tanh(_dot(h, win_ref[:, width:])), vg_ref[...]).astype(BF16)
        for g in range(CM_GROUPS):
            cols = slice(g * HEAD_DIM, (g + 1) * HEAD_DIM)
            vg = jnp.concatenate([v[ch * CHUNK:(ch + 1) * CHUNK, cols] for ch in range(n_chunks)], axis=1)
            mixed = _dot(ws_ref[g], vg)
            for ch in range(n_chunks):
                rows = slice(ch * CHUNK, (ch + 1) * CHUNK)
                sv = mixed[:, ch * CHUNK:(ch + 1) * CHUNK] + bs_ref[:, cols]
                uv_ref[r0 + ch * CHUNK:r0 + (ch + 1) * CHUNK, cols] = (u[rows, cols] * sv).astype(BF16)
    x1 = x_ref[...] + m[2:3] * _dot(uv_ref[...], wout_ref[...])
    o_ref[...] = _ffn_tail(x1, m, n2g_ref, w1_ref, w2_ref)


def _odd_call(xs, mod, layer, mod_row, n1g, n2g, w_in, v_g, w_s, b_s, w_out, w1, w2, name):
    b, l, d = xs.shape
    tm = min(ROW_TILE, l)
    width = w_out.shape[0]
    bias = jnp.repeat(b_s.T, width // CM_GROUPS, axis=1)
    row = pl.BlockSpec((None, tm, d), lambda bi, i: (bi, i, 0))
    return pl.pallas_call(
        _odd_kernel,
        grid=(b, l // tm),
        in_specs=[row, pl.BlockSpec((None, None, 6, d), lambda bi, i: (layer, mod_row(bi), 0, 0)),
                  _resident((1, d)), _resident((1, d)), _resident(w_in.shape), _resident((1, width)),
                  _resident(w_s.shape), _resident(bias.shape), _resident(w_out.shape),
                  _resident(w1.shape), _resident(w2.shape)],
        out_specs=row,
        out_shape=jax.ShapeDtypeStruct((b, l, d), F32),
        scratch_shapes=[pltpu.VMEM((tm, width), BF16)],
        compiler_params=_params(2),
        name=name,
    )(xs, mod, n1g.reshape(1, d), n2g.reshape(1, d), w_in, v_g.reshape(1, width), w_s, bias, w_out, w1, w2)


def _rope_tables(l):
    rows = l // GRID_W
    r = jnp.repeat(jnp.arange(rows, dtype=F32), GRID_W)
    c = jnp.tile(jnp.arange(GRID_W, dtype=F32), rows)
    n_freq = HEAD_DIM // 4
    inv = ROPE_BASE ** (-jnp.arange(n_freq, dtype=F32) / n_freq)
    ang = jnp.concatenate([r[:, None] * inv[None, :], c[:, None] * inv[None, :]], axis=-1)
    cos, sin = jnp.cos(ang), jnp.sin(ang)
    return jnp.concatenate([cos, cos], axis=-1), jnp.concatenate([-sin, sin], axis=-1)


def kernel(x, c, ctx, c_ctx, mod_w, mod_b, norm1_g, norm2_g, ab_w_in, ab_w_out, ret_decay, att_q_norm_g,
           att_k_norm_g, cm_w_in, cm_v_norm_g, cm_w_s, cm_b_s, cm_w_out, ff_w1, ff_w2):
    batch, seq, d = x.shape
    depth = mod_w.shape[0]
    assert seq % CHUNK == 0 and ctx.shape[1] % CHUNK == 0 and seq % GRID_W == 0
    assert ab_w_in.shape[2] == AB_HEADS * HEAD_DIM

    pad = (-(batch + 1)) % 8
    c_rows = jnp.concatenate([c, c_ctx[None, :], jnp.zeros((pad, d), F32)], axis=0)
    mod = _mod_call(c_rows, mod_w, mod_b).reshape(depth, batch + 1 + pad, 6, d)
    lat_row = lambda bi: bi
    ctx_row = lambda bi: batch

    rope_tabs = _rope_tables(seq)
    h_stream = ctx
    for l in range(depth):
        last = l == depth - 1
        i = l // 2
        w1, w2 = ff_w1[l].astype(BF16), ff_w2[l].astype(BF16)
        if l % 2 == 0:
            w_in, w_out = ab_w_in[i].astype(BF16), ab_w_out[i].astype(BF16)
            log_gamma = jax.nn.log_sigmoid(ret_decay[i].astype(F32))
            p_lat = _proj_call(x, mod, l, lat_row, norm1_g[l], w_in, att_q_norm_g[i], att_k_norm_g[i], rope_tabs)
            p_ctx = _proj_call(h_stream, mod, l, ctx_row, norm1_g[l], w_in, att_q_norm_g[i], att_k_norm_g[i], None)
            ret_l, ret_c = _ret_call(p_lat[0], p_ctx[0], log_gamma)
            att_l = _attn_call(p_lat[0], p_ctx, p_lat)
            x = _outffn_call(ret_l, att_l, x, mod, l, lat_row, norm2_g[l], w_out, w1, w2, "outffn_lat")
            if not last:
                att_c = _attn_call(p_ctx[0], p_ctx, None)
                h_stream = _outffn_call(ret_c, att_c, h_stream, mod, l, ctx_row, norm2_g[l], w_out, w1, w2,
                                        "outffn_ctx")
        else:
            cm = (cm_w_in[i].astype(BF16), cm_v_norm_g[i], cm_w_s[i].astype(BF16), cm_b_s[i],
                  cm_w_out[i].astype(BF16), w1, w2)
            x = _odd_call(x, mod, l, lat_row, norm1_g[l], norm2_g[l], *cm, "odd_lat")
            if not last:
                h_stream = _odd_call(h_stream, mod, l, ctx_row, norm1_g[l], norm2_g[l], *cm, "odd_ctx")
    return x
```

```python
import functools
import math

import jax
import jax.numpy as jnp
from jax import lax
from jax.experimental import pallas as pl
from jax.experimental.pallas import tpu as pltpu

F32 = jnp.float32
BF16 = jnp.bfloat16

EPS = 1e-6
HEAD_DIM = 128
RET_HEADS = 4
ATT_HEADS = 4
ATT_KV_HEADS = 2
ATT_GROUP = ATT_HEADS // ATT_KV_HEADS
CHUNK = 128
CM_GROUPS = 8
GRID_W = 64
ROPE_BASE = 10000.0
OFF_RQ, OFF_RK, OFF_RV, OFF_RG = 0, RET_HEADS, 2 * RET_HEADS, 3 * RET_HEADS
OFF_AQ = 4 * RET_HEADS
OFF_AK = OFF_AQ + ATT_HEADS
OFF_AV = OFF_AK + ATT_KV_HEADS
AB_HEADS = OFF_AV + ATT_KV_HEADS

V7X_VMEM_LIMIT_BYTES = 56 * 1024 * 1024
ROW_TILE = 512
ATT_Q_TILE = 1024
ATT_ROW_TILE = 256
ATT_KEY_TILE = 16
MOD_COL_TILE = 1536
LOG2E = math.log2(math.e)


def _params(n_axes):
    return pltpu.CompilerParams(dimension_semantics=("arbitrary",) * n_axes,
                                vmem_limit_bytes=V7X_VMEM_LIMIT_BYTES)


def _resident(shape):
    return pl.BlockSpec(shape, lambda *_: (0,) * len(shape), pipeline_mode=pl.Buffered(1))


def _resident_layer(stacked, index):
    zeros = (0,) * (stacked.ndim - 1)
    return pl.BlockSpec((None,) + stacked.shape[1:], lambda *_: (index,) + zeros, pipeline_mode=pl.Buffered(1))


def _dot(a, b):
    return jnp.dot(a, b, preferred_element_type=F32)


def _dot_nt(a, b):
    return lax.dot_general(a, b, (((1,), (1,)), ((), ())), preferred_element_type=F32)


def _dot_tn(a, b):
    return lax.dot_general(a, b, (((0,), (0,)), ((), ())), preferred_element_type=F32)


def _rms(x, g):
    return x * lax.rsqrt(jnp.mean(x * x, axis=-1, keepdims=True) + EPS) * g


def _norm_affine(x, gain, shift):
    return x * lax.rsqrt(jnp.mean(x * x, axis=-1, keepdims=True) + EPS) * gain + shift


def _silu(x):
    return x / (1.0 + jnp.exp2(x * (-LOG2E)))


def _gelu_tanh(x):
    a = -2.0 * math.sqrt(2.0 / math.pi) * LOG2E
    return x / (1.0 + jnp.exp2(x * (a + (a * 0.044715) * (x * x))))


def _mod_kernel(c_ref, w_ref, b_ref, o_ref):
    s = _silu(c_ref[...]).astype(BF16)
    o_ref[...] = _dot(s, w_ref[...].astype(BF16)) + b_ref[...]


def _mod_call(c_rows, mod_w, mod_b):
    depth, d, n = mod_w.shape
    r = c_rows.shape[0]
    tn = MOD_COL_TILE if n % MOD_COL_TILE == 0 else n
    return pl.pallas_call(
        _mod_kernel,
        grid=(depth, n // tn),
        in_specs=[pl.BlockSpec((r, d), lambda l, j: (0, 0)),
                  pl.BlockSpec((None, d, tn), lambda l, j: (l, 0, j)),
                  pl.BlockSpec((None, 1, tn), lambda l, j: (l, 0, j))],
        out_specs=pl.BlockSpec((None, r, tn), lambda l, j: (l, 0, j)),
        out_shape=jax.ShapeDtypeStruct((depth, r, n), F32),
        compiler_params=_params(2),
        name="mod",
    )(c_rows, mod_w, mod_b.reshape(depth, 1, n))


def _proj_kernel(*refs, rope):
    if rope:
        x_ref, mod_ref, g_ref, w_ref, qg_ref, kg_ref, cs_ref, sn_ref, o_ref, vt_ref = refs
    else:
        x_ref, mod_ref, g_ref, w_ref, qg_ref, kg_ref, o_ref, vt_ref = refs
    m = mod_ref[...]
    gm = g_ref[...] * (1.0 + m[1:2])
    scale = HEAD_DIM ** -0.5
    qg = qg_ref[...] * (scale * LOG2E)
    kg = kg_ref[...]
    tm = x_ref.shape[0]
    sub = tm // 2 if tm % (2 * CHUNK) == 0 else tm

    for r0 in range(0, tm, sub):
        rows = slice(r0, r0 + sub)
        h = _norm_affine(x_ref[rows, :], gm, m[0:1]).astype(BF16)
        if rope:
            cs, sn = cs_ref[rows, :], sn_ref[rows, :]
            cs_k, sn_k = cs * scale, sn * scale

        def rot(t, scaled=False):
            if not rope:
                return t * scale if scaled else t
            c, s = (cs_k, sn_k) if scaled else (cs, sn)
            return t * c + pltpu.roll(t, HEAD_DIM // 2, 1) * s

        def segment(first, count):
            y = _dot(h, w_ref[:, first * HEAD_DIM:(first + count) * HEAD_DIM])
            return [y[:, i * HEAD_DIM:(i + 1) * HEAD_DIM] for i in range(count)]

        for i, t in enumerate(segment(OFF_AQ, ATT_HEADS)):
            o_ref[OFF_AQ + i, rows, :] = rot(_rms(t, qg)).astype(BF16)
        for i, t in enumerate(segment(OFF_AK, ATT_KV_HEADS)):
            o_ref[OFF_AK + i, rows, :] = rot(_rms(t, kg)).astype(BF16)
        for i, t in enumerate(segment(OFF_AV, ATT_KV_HEADS)):
            vt_ref[i, :, rows] = t.T.astype(BF16)
        for i, t in enumerate(segment(OFF_RQ, RET_HEADS)):
            o_ref[OFF_RQ + i, rows, :] = rot(t).astype(BF16)
        for i, t in enumerate(segment(OFF_RK, RET_HEADS)):
            o_ref[OFF_RK + i, rows, :] = rot(t, scaled=True).astype(BF16)
        for i, t in enumerate(segment(OFF_RV, RET_HEADS)):
            o_ref[OFF_RV + i, rows, :] = t.astype(BF16)
        for i, t in enumerate(segment(OFF_RG, RET_HEADS)):
            o_ref[OFF_RG + i, rows, :] = t.astype(BF16)


def _proj_call(xs, mod, layer, mod_row, norm_g, w_in, q_g, k_g, rope_tabs):
    b, l, d = xs.shape
    tm = min(ROW_TILE, l)
    rope = rope_tabs is not None
    in_specs = [pl.BlockSpec((None, tm, d), lambda bi, i: (bi, i, 0)),
                pl.BlockSpec((None, None, 6, d), lambda bi, i: (layer, mod_row(bi), 0, 0)),
                _resident((1, d)), _resident_layer(*w_in),
                _resident((1, HEAD_DIM)), _resident((1, HEAD_DIM))]
    args = [xs, mod, norm_g.reshape(1, d), w_in[0], q_g.reshape(1, HEAD_DIM), k_g.reshape(1, HEAD_DIM)]
    if rope:
        in_specs += [pl.BlockSpec((tm, HEAD_DIM), lambda bi, i: (i, 0))] * 2
        args += list(rope_tabs)
    return pl.pallas_call(
        functools.partial(_proj_kernel, rope=rope),
        grid=(b, l // tm),
        in_specs=in_specs,
        out_specs=[pl.BlockSpec((None, OFF_AV, tm, HEAD_DIM), lambda bi, i: (bi, 0, i, 0)),
                   pl.BlockSpec((None, ATT_KV_HEADS, HEAD_DIM, tm), lambda bi, i: (bi, 0, 0, i))],
        out_shape=[jax.ShapeDtypeStruct((b, OFF_AV, l, HEAD_DIM), BF16),
                   jax.ShapeDtypeStruct((b, ATT_KV_HEADS, HEAD_DIM, l), BF16)],
        compiler_params=_params(2),
        name="proj_lat" if rope else "proj_ctx",
    )(*args)


def _ret_kernel(lg_ref, ql_ref, kl_ref, vl_ref, gl_ref, qc_ref, kc_ref, vc_ref, gc_ref,
                ol_ref, oc_ref, kv_ref, st_ref, lhs_ref):
    c = CHUNK
    n_ctx = qc_ref.shape[0] // c
    n_lat = ql_ref.shape[0] // c
    head = pl.program_id(1)
    lgf = lg_ref[0, head]
    lgb = lg_ref[1, head]
    row = lax.broadcasted_iota(jnp.int32, (c, HEAD_DIM), 0).astype(F32)
    rel = (lax.broadcasted_iota(jnp.int32, (c, c), 0) - lax.broadcasted_iota(jnp.int32, (c, c), 1)).astype(F32)
    dmat = (jnp.where(rel >= 0, jnp.exp(jnp.maximum(rel, 0.0) * lgf), 0.0)
            + jnp.where(rel <= 0, jnp.exp(jnp.maximum(-rel, 0.0) * lgb), 0.0))
    kdf = jnp.exp((c - 1.0 - row) * lgf).astype(BF16)
    kdb = jnp.exp(row * lgb).astype(BF16)
    qdf = jnp.exp((row + 1.0) * lgf).astype(BF16)
    qdb = jnp.exp((c - row) * lgb).astype(BF16)
    cdf = jnp.exp(jnp.full((HEAD_DIM, HEAD_DIM), c, F32) * lgf)
    cdb = jnp.exp(jnp.full((HEAD_DIM, HEAD_DIM), c, F32) * lgb)

    chunks = ([(qc_ref, kc_ref, vc_ref, gc_ref, oc_ref, n) for n in range(n_ctx)]
              + [(ql_ref, kl_ref, vl_ref, gl_ref, ol_ref, n) for n in range(n_lat)])
    n_all = len(chunks)

    for t, (q_ref, k_ref, v_ref, _, _, n) in enumerate(chunks):
        rows = slice(n * c, (n + 1) * c)
        k = k_ref[rows, :]
        v = v_ref[rows, :]
        kv_ref[t] = _dot_tn(k, jnp.concatenate([v * kdf, v * kdb], axis=1))
        q = q_ref[rows, :]
        a = (_dot_nt(q, k) * dmat).astype(BF16)
        lhs_ref[t] = jnp.concatenate([a, q * qdf, q * qdb], axis=1)

    state = jnp.zeros((HEAD_DIM, HEAD_DIM), F32)
    for t in range(n_all):
        st_ref[t, :HEAD_DIM, :] = state.astype(BF16)
        state = cdf * state + kv_ref[t, :, :HEAD_DIM]
    state = jnp.zeros((HEAD_DIM, HEAD_DIM), F32)
    for t in list(reversed(range(n_ctx))) + list(reversed(range(n_ctx, n_all))):
        st_ref[t, HEAD_DIM:, :] = state.astype(BF16)
        state = cdb * state + kv_ref[t, :, HEAD_DIM:]

    for t, (_, _, v_ref, g_ref, o_ref, n) in enumerate(chunks):
        rows = slice(n * c, (n + 1) * c)
        o = _dot(lhs_ref[t], jnp.concatenate([v_ref[rows, :], st_ref[t]], axis=0))
        mu = jnp.mean(o, axis=-1, keepdims=True)
        dev = o - mu
        y = dev * lax.rsqrt(jnp.mean(dev * dev, axis=-1, keepdims=True) + EPS)
        o_ref[rows, :] = (y * _silu(g_ref[rows, :].astype(F32))).astype(BF16)


def _ret_call(p_lat, p_ctx, log_gamma):
    b, _, l, _ = p_lat.shape
    lc = p_ctx.shape[2]
    n_all = (l + lc) // CHUNK

    def head_spec(rows, off):
        return pl.BlockSpec((None, None, rows, HEAD_DIM), lambda bi, h: (bi, off + h, 0, 0))

    return pl.pallas_call(
        _ret_kernel,
        grid=(b, RET_HEADS),
        in_specs=[pl.BlockSpec(memory_space=pltpu.SMEM)]
        + [head_spec(l, off) for off in (OFF_RQ, OFF_RK, OFF_RV, OFF_RG)]
        + [head_spec(lc, off) for off in (OFF_RQ, OFF_RK, OFF_RV, OFF_RG)],
        out_specs=[head_spec(l, 0), head_spec(lc, 0)],
        out_shape=[jax.ShapeDtypeStruct((b, RET_HEADS, l, HEAD_DIM), BF16),
                   jax.ShapeDtypeStruct((b, RET_HEADS, lc, HEAD_DIM), BF16)],
        scratch_shapes=[pltpu.VMEM((n_all, HEAD_DIM, 2 * HEAD_DIM), F32),
                        pltpu.VMEM((n_all, 2 * HEAD_DIM, HEAD_DIM), BF16),
                        pltpu.VMEM((n_all, CHUNK, CHUNK + 2 * HEAD_DIM), BF16)],
        compiler_params=_params(2),
        name="retention",
    )(log_gamma, p_lat, p_lat, p_lat, p_lat, p_ctx, p_ctx, p_ctx, p_ctx)


def _fold_rows(x, rows, op):
    parts = [x[i:i + rows, :] for i in range(0, x.shape[0], rows)]
    while len(parts) > 1:
        parts = [op(parts[i], parts[i + 1]) for i in range(0, len(parts) - 1, 2)] + parts[len(parts) & ~1:]
    return parts[0]


def _attn_kernel(*refs, with_lat):
    if with_lat:
        q_ref, kc_ref, vtc_ref, kl_ref, vtl_ref, o_ref, s_ref, p_ref = refs
    else:
        q_ref, kc_ref, vtc_ref, o_ref, s_ref, p_ref = refs
        kl_ref = vtl_ref = None
    tq = q_ref.shape[1]
    lc = kc_ref.shape[0]
    n_keys = s_ref.shape[1]
    rt = s_ref.shape[2] // ATT_GROUP
    kt = ATT_KEY_TILE
    sub = 8

    n_units = tq // rt
    piece = 4 * HEAD_DIM
    pieces = [(0, lc)] + [(s, min(s + piece, n_keys)) for s in range(lc, n_keys, piece)]

    def keys(ref_c, ref_l, lo, hi, axis):
        ref, lo, hi = (ref_c, lo, hi) if lo < lc else (ref_l, lo - lc, hi - lc)
        return ref[lo:hi, :] if axis == 0 else ref[:, lo:hi]

    def unit_rows(u):
        return pl.ds(u * rt, rt) if isinstance(u, int) else pl.ds(pl.multiple_of(u * rt, rt), rt)

    def scores(u, slot):
        q2 = jnp.concatenate([q_ref[g, unit_rows(u), :] for g in range(ATT_GROUP)], axis=0)
        for lo, hi in pieces:
            s_ref[slot, lo:hi, :] = _dot_nt(keys(kc_ref, kl_ref, lo, hi, 0), q2)

    def row_max(slot):
        m = None
        for start in range(0, n_keys, kt):
            t = _fold_rows(s_ref[slot, start:start + kt, :], sub, jnp.maximum)
            m = t if m is None else jnp.maximum(m, t)
        return jnp.max(m, axis=0, keepdims=True)

    def exp_sum(slot, m):
        den = None
        for start in range(0, n_keys, kt):
            p = jnp.exp2(s_ref[slot, start:start + kt, :] - m)
            t = _fold_rows(p, sub, jnp.add)
            den = t if den is None else den + t
            p_ref[slot, start:start + kt, :] = p.astype(BF16)
        return den

    def pv(u, slot, den):
        acc = None
        for lo, hi in pieces:
            part = _dot(keys(vtc_ref, vtl_ref, lo, hi, 1), p_ref[slot, lo:hi, :])
            acc = part if acc is None else acc + part
        out = acc / jnp.sum(den, axis=0, keepdims=True)
        for g in range(ATT_GROUP):
            o_ref[g, unit_rows(u), :] = out[:, g * rt:(g + 1) * rt].T.astype(BF16)

    scores(0, 0)
    den_prev = None
    for u in range(n_units):
        slot = u % 2
        if u + 1 < n_units:
            scores(u + 1, 1 - slot)
        den = exp_sum(slot, row_max(slot))
        if u > 0:
            pv(u - 1, 1 - slot, den_prev)
        den_prev = den
    pv(n_units - 1, (n_units - 1) % 2, den_prev)


def _attn_call(p_q, ctx_kv, lat_kv):
    b, _, lq, _ = p_q.shape
    lc = ctx_kv[0].shape[2]
    tq = min(ATT_Q_TILE, lq)
    rt = min(ATT_ROW_TILE, tq)
    with_lat = lat_kv is not None
    n_keys = lc + (lat_kv[0].shape[2] if with_lat else 0)

    def k_spec(rows):
        return pl.BlockSpec((None, None, rows, HEAD_DIM), lambda bi, kv, i: (bi, OFF_AK + kv, 0, 0))

    def vt_spec(rows):
        return pl.BlockSpec((None, None, HEAD_DIM, rows), lambda bi, kv, i: (bi, kv, 0, 0))

    in_specs = [pl.BlockSpec((None, ATT_GROUP, tq, HEAD_DIM), lambda bi, kv, i: (bi, OFF_AQ // ATT_GROUP + kv, i, 0)),
                k_spec(lc), vt_spec(lc)]
    args = [p_q, *ctx_kv]
    if with_lat:
        in_specs += [k_spec(n_keys - lc), vt_spec(n_keys - lc)]
        args += list(lat_kv)
    return pl.pallas_call(
        functools.partial(_attn_kernel, with_lat=with_lat),
        grid=(b, ATT_KV_HEADS, lq // tq),
        in_specs=in_specs,
        out_specs=pl.BlockSpec((None, ATT_GROUP, tq, HEAD_DIM), lambda bi, kv, i: (bi, kv, i, 0)),
        out_shape=jax.ShapeDtypeStruct((b, ATT_HEADS, lq, HEAD_DIM), BF16),
        scratch_shapes=[pltpu.VMEM((2, n_keys, ATT_GROUP * rt), F32),
                        pltpu.VMEM((2, n_keys, ATT_GROUP * rt), BF16)],
        compiler_params=_params(3),
        name="attn_lat" if with_lat else "attn_ctx",
    )(*args)


def _ffn_tail(x1, m, n2g_ref, w1_ref, w2_ref):
    h = _norm_affine(x1, n2g_ref[...] * (1.0 + m[4:5]), m[3:4]).astype(BF16)
    hidden = w1_ref.shape[1]
    th = min(1024, hidden)
    acc = None
    for j in range(hidden // th):
        a = jnp.maximum(_dot(h, w1_ref[:, j * th:(j + 1) * th]), 0.0)
        part = _dot((a * a).astype(BF16), w2_ref[j * th:(j + 1) * th, :])
        acc = part if acc is None else acc + part
    return x1 + m[5:6] * acc


def _outffn_kernel(ret_ref, att_ref, x_ref, mod_ref, n2g_ref, wo_ref, w1_ref, w2_ref, o_ref):
    m = mod_ref[...]
    tm = x_ref.shape[0]
    sub = tm // 2 if tm % (2 * CHUNK) == 0 else tm
    x1 = []
    for r0 in range(0, tm, sub):
        rows = slice(r0, r0 + sub)
        mixed = jnp.concatenate([ret_ref[h, rows, :] for h in range(ret_ref.shape[0])]
                                + [att_ref[h, rows, :] for h in range(att_ref.shape[0])], axis=1)
        x1.append(x_ref[rows, :] + m[2:3] * _dot(mixed, wo_ref[...]))
    for r0, x1_half in zip(range(0, tm, sub), x1):
        o_ref[r0:r0 + sub, :] = _ffn_tail(x1_half, m, n2g_ref, w1_ref, w2_ref)


def _outffn_call(ret, att, xs, mod, layer, mod_row, n2g, w_out, w1, w2, name):
    b, l, d = xs.shape
    tm = min(ROW_TILE, l)
    row = pl.BlockSpec((None, tm, d), lambda bi, i: (bi, i, 0))
    heads = lambda a: pl.BlockSpec((None, a.shape[1], tm, HEAD_DIM), lambda bi, i: (bi, 0, i, 0))
    return pl.pallas_call(
        _outffn_kernel,
        grid=(b, l // tm),
        in_specs=[heads(ret), heads(att), row,
                  pl.BlockSpec((None, None, 6, d), lambda bi, i: (layer, mod_row(bi), 0, 0)),
                  _resident((1, d)), _resident_layer(*w_out), _resident_layer(*w1), _resident_layer(*w2)],
        out_specs=row,
        out_shape=jax.ShapeDtypeStruct((b, l, d), F32),
        compiler_params=_params(2),
        name=name,
    )(ret, att, xs, mod, n2g.reshape(1, d), w_out[0], w1[0], w2[0])


def _odd_kernel(x_ref, mod_ref, n1g_ref, n2g_ref, win_ref, vg_ref, ws_ref, bs_ref, wout_ref,
                w1_ref, w2_ref, o_ref, uv_ref):
    m = mod_ref[...]
    tm = x_ref.shape[0]
    width = wout_ref.shape[0]
    gain = n1g_ref[...] * (1.0 + m[1:2])
    sub = 2 * CHUNK if tm % (2 * CHUNK) == 0 else CHUNK
    n_chunks = sub // CHUNK
    pre = []
    for r0 in range(0, tm, sub):
        h = _norm_affine(x_ref[r0:r0 + sub, :], gain, m[0:1]).astype(BF16)
        pre.append((_dot(h, win_ref[:, width:]), _dot(h, win_ref[:, :width])))
    for r0, (v_pre, u_pre) in zip(range(0, tm, sub), pre):
        v = _rms(_gelu_tanh(v_pre), vg_ref[...]).astype(BF16)
        u = _gelu_tanh(u_pre)
        for g in range(CM_GROUPS):
            cols = slice(g * HEAD_DIM, (g + 1) * HEAD_DIM)
            vg = jnp.concatenate([v[ch * CHUNK:(ch + 1) * CHUNK, cols] for ch in range(n_chunks)], axis=1)
            mixed = _dot(ws_ref[g], vg)
            for ch in range(n_chunks):
                rows = slice(ch * CHUNK, (ch + 1) * CHUNK)
                sv = mixed[:, ch * CHUNK:(ch + 1) * CHUNK] + bs_ref[:, cols]
                uv_ref[r0 + ch * CHUNK:r0 + (ch + 1) * CHUNK, cols] = (u[rows, cols] * sv).astype(BF16)
    x1 = x_ref[...] + m[2:3] * _dot(uv_ref[...], wout_ref[...])
    o_ref[...] = _ffn_tail(x1, m, n2g_ref, w1_ref, w2_ref)


def _odd_call(xs, mod, layer, mod_row, n1g, n2g, w_in, v_g, w_s, b_s, w_out, w1, w2, name):
    b, l, d = xs.shape
    tm = min(ROW_TILE, l)
    width = w_out[0].shape[1]
    bias = jnp.repeat(b_s.T, width // CM_GROUPS, axis=1)
    row = pl.BlockSpec((None, tm, d), lambda bi, i: (bi, i, 0))
    return pl.pallas_call(
        _odd_kernel,
        grid=(b, l // tm),
        in_specs=[row, pl.BlockSpec((None, None, 6, d), lambda bi, i: (layer, mod_row(bi), 0, 0)),
                  _resident((1, d)), _resident((1, d)), _resident_layer(*w_in), _resident((1, width)),
                  _resident_layer(*w_s), _resident(bias.shape), _resident_layer(*w_out),
                  _resident_layer(*w1), _resident_layer(*w2)],
        out_specs=row,
        out_shape=jax.ShapeDtypeStruct((b, l, d), F32),
        scratch_shapes=[pltpu.VMEM((tm, width), BF16)],
        compiler_params=_params(2),
        name=name,
    )(xs, mod, n1g.reshape(1, d), n2g.reshape(1, d), w_in[0], v_g.reshape(1, width), w_s[0], bias,
      w_out[0], w1[0], w2[0])


def _rope_tables(l):
    rows = l // GRID_W
    r = jnp.repeat(jnp.arange(rows, dtype=F32), GRID_W)
    c = jnp.tile(jnp.arange(GRID_W, dtype=F32), rows)
    n_freq = HEAD_DIM // 4
    inv = ROPE_BASE ** (-jnp.arange(n_freq, dtype=F32) / n_freq)
    ang = jnp.concatenate([r[:, None] * inv[None, :], c[:, None] * inv[None, :]], axis=-1)
    cos, sin = jnp.cos(ang), jnp.sin(ang)
    return jnp.concatenate([cos, cos], axis=-1), jnp.concatenate([-sin, sin], axis=-1)


def kernel(x, c, ctx, c_ctx, mod_w, mod_b, norm1_g, norm2_g, ab_w_in, ab_w_out, ret_decay, att_q_norm_g,
           att_k_norm_g, cm_w_in, cm_v_norm_g, cm_w_s, cm_b_s, cm_w_out, ff_w1, ff_w2):
    batch, seq, d = x.shape
    depth = mod_w.shape[0]
    assert seq % CHUNK == 0 and ctx.shape[1] % CHUNK == 0 and seq % GRID_W == 0
    assert ab_w_in.shape[2] == AB_HEADS * HEAD_DIM

    pad = (-(batch + 1)) % 8
    c_rows = jnp.concatenate([c, c_ctx[None, :], jnp.zeros((pad, d), F32)], axis=0)
    mod = _mod_call(c_rows, mod_w, mod_b).reshape(depth, batch + 1 + pad, 6, d)
    lat_row = lambda bi: bi
    ctx_row = lambda bi: batch

    rope_tabs = _rope_tables(seq)
    ff_w1, ff_w2, ab_w_in, ab_w_out, cm_w_in, cm_w_s, cm_w_out = (
        w.astype(BF16) for w in (ff_w1, ff_w2, ab_w_in, ab_w_out, cm_w_in, cm_w_s, cm_w_out))
    h_stream = ctx
    for l in range(depth):
        last = l == depth - 1
        i = l // 2
        w1, w2 = (ff_w1, l), (ff_w2, l)
        if l % 2 == 0:
            w_in, w_out = (ab_w_in, i), (ab_w_out, i)
            log_gamma = jax.nn.log_sigmoid(ret_decay[i].astype(F32))
            p_lat = _proj_call(x, mod, l, lat_row, norm1_g[l], w_in, att_q_norm_g[i], att_k_norm_g[i], rope_tabs)
            p_ctx = _proj_call(h_stream, mod, l, ctx_row, norm1_g[l], w_in, att_q_norm_g[i], att_k_norm_g[i], None)
            ret_l, ret_c = _ret_call(p_lat[0], p_ctx[0], log_gamma)
            att_l = _attn_call(p_lat[0], p_ctx, p_lat)
            x = _outffn_call(ret_l, att_l, x, mod, l, lat_row, norm2_g[l], w_out, w1, w2, "outffn_lat")
            if not last:
                att_c = _attn_call(p_ctx[0], p_ctx, None)
                h_stream = _outffn_call(ret_c, att_c, h_stream, mod, l, ctx_row, norm2_g[l], w_out, w1, w2,
                                        "outffn_ctx")
        else:
            cm = ((cm_w_in, i), cm_v_norm_g[i], (cm_w_s, i), cm_b_s[i], (cm_w_out, i), w1, w2)
            x = _odd_call(x, mod, l, lat_row, norm1_g[l], norm2_g[l], *cm, "odd_lat")
            if not last:
                h_stream = _odd_call(h_stream, mod, l, ctx_row, norm1_g[l], norm2_g[l], *cm, "odd_ctx")
    return x
```

```python
import functools
import math

import jax
import jax.numpy as jnp
from jax import lax
from jax.experimental import pallas as pl
from jax.experimental.pallas import tpu as pltpu

F32 = jnp.float32
BF16 = jnp.bfloat16

EPS = 1e-6
HEAD_DIM = 128
RET_HEADS = 4
ATT_HEADS = 4
ATT_KV_HEADS = 2
ATT_GROUP = ATT_HEADS // ATT_KV_HEADS
CHUNK = 128
CM_GROUPS = 8
GRID_W = 64
ROPE_BASE = 10000.0
OFF_RQ, OFF_RK, OFF_RV, OFF_RG = 0, RET_HEADS, 2 * RET_HEADS, 3 * RET_HEADS
OFF_AQ = 4 * RET_HEADS
OFF_AK = OFF_AQ + ATT_HEADS
OFF_AV = OFF_AK + ATT_KV_HEADS
AB_HEADS = OFF_AV + ATT_KV_HEADS

V7X_VMEM_LIMIT_BYTES = 56 * 1024 * 1024
ROW_TILE = 512
PROJ_ROW_TILE = 1024
PROJ_SUB_TILE = 256
ATT_Q_TILE = 1024
ATT_ROW_TILE = 256
ATT_KEY_TILE = 16
MOD_COL_TILE = 1536
LOG2E = math.log2(math.e)


def _params(n_axes):
    return pltpu.CompilerParams(dimension_semantics=("arbitrary",) * n_axes,
                                vmem_limit_bytes=V7X_VMEM_LIMIT_BYTES)


def _row_tile(rows, preferred):
    assert rows % CHUNK == 0
    return max(t for t in range(CHUNK, min(rows, preferred) + 1, CHUNK) if rows % t == 0)


def _resident(shape):
    return pl.BlockSpec(shape, lambda *_: (0,) * len(shape), pipeline_mode=pl.Buffered(1))


def _resident_layer(stacked, index):
    zeros = (0,) * (stacked.ndim - 1)
    return pl.BlockSpec((None,) + stacked.shape[1:], lambda *_: (index,) + zeros, pipeline_mode=pl.Buffered(1))


def _dot(a, b):
    return jnp.dot(a, b, preferred_element_type=F32)


def _dot_nt(a, b):
    return lax.dot_general(a, b, (((1,), (1,)), ((), ())), preferred_element_type=F32)


def _dot_tn(a, b):
    return lax.dot_general(a, b, (((0,), (0,)), ((), ())), preferred_element_type=F32)


def _rms(x, g):
    return x * lax.rsqrt(jnp.mean(x * x, axis=-1, keepdims=True) + EPS) * g


def _norm_affine(x, gain, shift):
    return x * lax.rsqrt(jnp.mean(x * x, axis=-1, keepdims=True) + EPS) * gain + shift


def _silu(x):
    return x / (1.0 + jnp.exp2(x * (-LOG2E)))


def _gelu_tanh(x):
    a = -2.0 * math.sqrt(2.0 / math.pi) * LOG2E
    return x / (1.0 + jnp.exp2(x * (a + (a * 0.044715) * (x * x))))


def _mod_kernel(c_ref, w_ref, b_ref, o_ref):
    s = _silu(c_ref[...]).astype(BF16)
    o_ref[...] = _dot(s, w_ref[...].astype(BF16)) + b_ref[...]


def _mod_call(c_rows, mod_w, mod_b):
    depth, d, n = mod_w.shape
    r = c_rows.shape[0]
    tn = MOD_COL_TILE if n % MOD_COL_TILE == 0 else n
    return pl.pallas_call(
        _mod_kernel,
        grid=(depth, n // tn),
        in_specs=[pl.BlockSpec((r, d), lambda l, j: (0, 0)),
                  pl.BlockSpec((None, d, tn), lambda l, j: (l, 0, j)),
                  pl.BlockSpec((None, 1, tn), lambda l, j: (l, 0, j))],
        out_specs=pl.BlockSpec((None, r, tn), lambda l, j: (l, 0, j)),
        out_shape=jax.ShapeDtypeStruct((depth, r, n), F32),
        compiler_params=_params(2),
        name="mod",
    )(c_rows, mod_w, mod_b.reshape(depth, 1, n))


def _proj_kernel(*refs, rope):
    if rope:
        x_ref, mod_ref, g_ref, w_ref, qg_ref, kg_ref, cs_ref, sn_ref, o_ref, vt_ref = refs
    else:
        x_ref, mod_ref, g_ref, w_ref, qg_ref, kg_ref, o_ref, vt_ref = refs
    m = mod_ref[...]
    gm = g_ref[...] * (1.0 + m[1:2])
    scale = HEAD_DIM ** -0.5
    qg = qg_ref[...] * (scale * LOG2E)
    kg = kg_ref[...]
    tm = x_ref.shape[0]
    sub = PROJ_SUB_TILE if tm % PROJ_SUB_TILE == 0 else tm

    for r0 in range(0, tm, sub):
        rows = slice(r0, r0 + sub)
        h = _norm_affine(x_ref[rows, :], gm, m[0:1]).astype(BF16)
        if rope:
            cs, sn = cs_ref[rows, :], sn_ref[rows, :]
            cs_k, sn_k = cs * scale, sn * scale

        def rot(t, scaled=False):
            if not rope:
                return t * scale if scaled else t
            c, s = (cs_k, sn_k) if scaled else (cs, sn)
            return t * c + pltpu.roll(t, HEAD_DIM // 2, 1) * s

        def segment(first, count):
            y = _dot(h, w_ref[:, first * HEAD_DIM:(first + count) * HEAD_DIM])
            return [y[:, i * HEAD_DIM:(i + 1) * HEAD_DIM] for i in range(count)]

        for i, t in enumerate(segment(OFF_AQ, ATT_HEADS)):
            o_ref[OFF_AQ + i, rows, :] = rot(_rms(t, qg)).astype(BF16)
        for i, t in enumerate(segment(OFF_AK, ATT_KV_HEADS)):
            o_ref[OFF_AK + i, rows, :] = rot(_rms(t, kg)).astype(BF16)
        for i, t in enumerate(segment(OFF_AV, ATT_KV_HEADS)):
            vt_ref[i, :, rows] = t.T.astype(BF16)
        for i, t in enumerate(segment(OFF_RQ, RET_HEADS)):
            o_ref[OFF_RQ + i, rows, :] = rot(t).astype(BF16)
        for i, t in enumerate(segment(OFF_RK, RET_HEADS)):
            o_ref[OFF_RK + i, rows, :] = rot(t, scaled=True).astype(BF16)
        for i, t in enumerate(segment(OFF_RV, RET_HEADS)):
            o_ref[OFF_RV + i, rows, :] = t.astype(BF16)
        for i, t in enumerate(segment(OFF_RG, RET_HEADS)):
            o_ref[OFF_RG + i, rows, :] = t.astype(BF16)


def _proj_call(xs, mod, layer, mod_row, norm_g, w_in, q_g, k_g, rope_tabs):
    b, l, d = xs.shape
    tm = _row_tile(l, PROJ_ROW_TILE)
    rope = rope_tabs is not None
    in_specs = [pl.BlockSpec((None, tm, d), lambda bi, i: (bi, i, 0)),
                pl.BlockSpec((None, None, 6, d), lambda bi, i: (layer, mod_row(bi), 0, 0)),
                _resident((1, d)), _resident_layer(*w_in),
                _resident((1, HEAD_DIM)), _resident((1, HEAD_DIM))]
    args = [xs, mod, norm_g.reshape(1, d), w_in[0], q_g.reshape(1, HEAD_DIM), k_g.reshape(1, HEAD_DIM)]
    if rope:
        in_specs += [pl.BlockSpec((tm, HEAD_DIM), lambda bi, i: (i, 0))] * 2
        args += list(rope_tabs)
    return pl.pallas_call(
        functools.partial(_proj_kernel, rope=rope),
        grid=(b, l // tm),
        in_specs=in_specs,
        out_specs=[pl.BlockSpec((None, OFF_AV, tm, HEAD_DIM), lambda bi, i: (bi, 0, i, 0)),
                   pl.BlockSpec((None, ATT_KV_HEADS, HEAD_DIM, tm), lambda bi, i: (bi, 0, 0, i))],
        out_shape=[jax.ShapeDtypeStruct((b, OFF_AV, l, HEAD_DIM), BF16),
                   jax.ShapeDtypeStruct((b, ATT_KV_HEADS, HEAD_DIM, l), BF16)],
        compiler_params=_params(2),
        name="proj_lat" if rope else "proj_ctx",
    )(*args)


def _ret_kernel(lg_ref, ql_ref, kl_ref, vl_ref, gl_ref, qc_ref, kc_ref, vc_ref, gc_ref,
                ol_ref, oc_ref, kv_ref, st_ref, lhs_ref):
    c = CHUNK
    n_ctx = qc_ref.shape[0] // c
    n_lat = ql_ref.shape[0] // c
    head = pl.program_id(1)
    lgf = lg_ref[0, head]
    lgb = lg_ref[1, head]
    row = lax.broadcasted_iota(jnp.int32, (c, HEAD_DIM), 0).astype(F32)
    rel = (lax.broadcasted_iota(jnp.int32, (c, c), 0) - lax.broadcasted_iota(jnp.int32, (c, c), 1)).astype(F32)
    dmat = (jnp.where(rel >= 0, jnp.exp(jnp.maximum(rel, 0.0) * lgf), 0.0)
            + jnp.where(rel <= 0, jnp.exp(jnp.maximum(-rel, 0.0) * lgb), 0.0))
    kdf = jnp.exp((c - 1.0 - row) * lgf).astype(BF16)
    kdb = jnp.exp(row * lgb).astype(BF16)
    qdf = jnp.exp((row + 1.0) * lgf).astype(BF16)
    qdb = jnp.exp((c - row) * lgb).astype(BF16)
    cdf = jnp.exp(jnp.full((HEAD_DIM, HEAD_DIM), c, F32) * lgf)
    cdb = jnp.exp(jnp.full((HEAD_DIM, HEAD_DIM), c, F32) * lgb)

    chunks = ([(qc_ref, kc_ref, vc_ref, gc_ref, oc_ref, n) for n in range(n_ctx)]
              + [(ql_ref, kl_ref, vl_ref, gl_ref, ol_ref, n) for n in range(n_lat)])
    n_all = len(chunks)

    for t, (q_ref, k_ref, v_ref, _, _, n) in enumerate(chunks):
        rows = slice(n * c, (n + 1) * c)
        k = k_ref[rows, :]
        v = v_ref[rows, :]
        kv_ref[t] = _dot_tn(k, jnp.concatenate([v * kdf, v * kdb], axis=1))
        q = q_ref[rows, :]
        a = (_dot_nt(q, k) * dmat).astype(BF16)
        lhs_ref[t] = jnp.concatenate([a, q * qdf, q * qdb], axis=1)

    state = jnp.zeros((HEAD_DIM, HEAD_DIM), F32)
    for t in range(n_all):
        st_ref[t, :HEAD_DIM, :] = state.astype(BF16)
        state = cdf * state + kv_ref[t, :, :HEAD_DIM]
    state = jnp.zeros((HEAD_DIM, HEAD_DIM), F32)
    for t in list(reversed(range(n_ctx))) + list(reversed(range(n_ctx, n_all))):
        st_ref[t, HEAD_DIM:, :] = state.astype(BF16)
        state = cdb * state + kv_ref[t, :, HEAD_DIM:]

    for t, (_, _, v_ref, g_ref, o_ref, n) in enumerate(chunks):
        rows = slice(n * c, (n + 1) * c)
        o = _dot(lhs_ref[t], jnp.concatenate([v_ref[rows, :], st_ref[t]], axis=0))
        mu = jnp.mean(o, axis=-1, keepdims=True)
        dev = o - mu
        y = dev * lax.rsqrt(jnp.mean(dev * dev, axis=-1, keepdims=True) + EPS)
        o_ref[rows, :] = (y * _silu(g_ref[rows, :].astype(F32))).astype(BF16)


def _ret_call(p_lat, p_ctx, log_gamma):
    b, _, l, _ = p_lat.shape
    lc = p_ctx.shape[2] // b
    n_all = (l + lc) // CHUNK

    def head_spec(rows, off):
        return pl.BlockSpec((None, None, rows, HEAD_DIM), lambda bi, h: (bi, off + h, 0, 0))

    def ctx_spec(off):
        return pl.BlockSpec((None, None, lc, HEAD_DIM), lambda bi, h: (0, off + h, bi, 0))

    return pl.pallas_call(
        _ret_kernel,
        grid=(b, RET_HEADS),
        in_specs=[pl.BlockSpec(memory_space=pltpu.SMEM)]
        + [head_spec(l, off) for off in (OFF_RQ, OFF_RK, OFF_RV, OFF_RG)]
        + [ctx_spec(off) for off in (OFF_RQ, OFF_RK, OFF_RV, OFF_RG)],
        out_specs=[head_spec(l, 0), ctx_spec(0)],
        out_shape=[jax.ShapeDtypeStruct((b, RET_HEADS, l, HEAD_DIM), BF16),
                   jax.ShapeDtypeStruct((1, RET_HEADS, b * lc, HEAD_DIM), BF16)],
        scratch_shapes=[pltpu.VMEM((n_all, HEAD_DIM, 2 * HEAD_DIM), F32),
                        pltpu.VMEM((n_all, 2 * HEAD_DIM, HEAD_DIM), BF16),
                        pltpu.VMEM((n_all, CHUNK, CHUNK + 2 * HEAD_DIM), BF16)],
        compiler_params=_params(2),
        name="retention",
    )(log_gamma, p_lat, p_lat, p_lat, p_lat, p_ctx, p_ctx, p_ctx, p_ctx)


def _fold_rows(x, rows, op):
    parts = [x[i:i + rows, :] for i in range(0, x.shape[0], rows)]
    while len(parts) > 1:
        parts = [op(parts[i], parts[i + 1]) for i in range(0, len(parts) - 1, 2)] + parts[len(parts) & ~1:]
    return parts[0]


def _attn_kernel(*refs, with_lat):
    if with_lat:
        q_ref, kc_ref, vtc_ref, kl_ref, vtl_ref, o_ref, s_ref, p_ref = refs
    else:
        q_ref, kc_ref, vtc_ref, o_ref, s_ref, p_ref = refs
        kl_ref = vtl_ref = None
    tq = q_ref.shape[1]
    lc = kc_ref.shape[0]
    n_keys = s_ref.shape[1]
    rt = s_ref.shape[2] // ATT_GROUP
    kt = ATT_KEY_TILE
    sub = 8

    n_units = tq // rt
    piece = 4 * HEAD_DIM
    pieces = [(0, lc)] + [(s, min(s + piece, n_keys)) for s in range(lc, n_keys, piece)]

    def keys(ref_c, ref_l, lo, hi, axis):
        ref, lo, hi = (ref_c, lo, hi) if lo < lc else (ref_l, lo - lc, hi - lc)
        return ref[lo:hi, :] if axis == 0 else ref[:, lo:hi]

    def unit_rows(u):
        return pl.ds(u * rt, rt) if isinstance(u, int) else pl.ds(pl.multiple_of(u * rt, rt), rt)

    def scores(u, slot):
        q2 = jnp.concatenate([q_ref[g, unit_rows(u), :] for g in range(ATT_GROUP)], axis=0)
        for lo, hi in pieces:
            s_ref[slot, lo:hi, :] = _dot_nt(keys(kc_ref, kl_ref, lo, hi, 0), q2)

    def row_max(slot):
        m = None
        for start in range(0, n_keys, kt):
            t = _fold_rows(s_ref[slot, start:start + kt, :], sub, jnp.maximum)
            m = t if m is None else jnp.maximum(m, t)
        return jnp.max(m, axis=0, keepdims=True)

    def exp_sum(slot, m):
        den = None
        for start in range(0, n_keys, kt):
            p = jnp.exp2(s_ref[slot, start:start + kt, :] - m)
            t = _fold_rows(p, sub, jnp.add)
            den = t if den is None else den + t
            p_ref[slot, start:start + kt, :] = p.astype(BF16)
        return den

    def pv(u, slot, den):
        acc = None
        for lo, hi in pieces:
            part = _dot(keys(vtc_ref, vtl_ref, lo, hi, 1), p_ref[slot, lo:hi, :])
            acc = part if acc is None else acc + part
        out = acc / jnp.sum(den, axis=0, keepdims=True)
        for g in range(ATT_GROUP):
            o_ref[g, unit_rows(u), :] = out[:, g * rt:(g + 1) * rt].T.astype(BF16)

    scores(0, 0)
    den_prev = None
    for u in range(n_units):
        slot = u % 2
        if u + 1 < n_units:
            scores(u + 1, 1 - slot)
        den = exp_sum(slot, row_max(slot))
        if u > 0:
            pv(u - 1, 1 - slot, den_prev)
        den_prev = den
    pv(n_units - 1, (n_units - 1) % 2, den_prev)


def _attn_call(batch, ctx_kv, lat_kv):
    lc = ctx_kv[0].shape[2] // batch
    with_lat = lat_kv is not None
    lq = lat_kv[0].shape[2] if with_lat else lc
    tq = _row_tile(lq, ATT_Q_TILE)
    rt = _row_tile(tq, ATT_ROW_TILE)
    n_q = lq // tq
    n_keys = lc + (lq if with_lat else 0)
    q_head = OFF_AQ // ATT_GROUP

    ctx_k = pl.BlockSpec((None, None, lc, HEAD_DIM), lambda bi, kv, i: (0, OFF_AK + kv, bi, 0))
    ctx_vt = pl.BlockSpec((None, None, HEAD_DIM, lc), lambda bi, kv, i: (0, kv, 0, bi))
    if with_lat:
        q_map = lambda bi, kv, i: (bi, q_head + kv, i, 0)
        o_map = lambda bi, kv, i: (bi, kv, i, 0)
        in_specs = [pl.BlockSpec((None, ATT_GROUP, tq, HEAD_DIM), q_map), ctx_k, ctx_vt,
                    pl.BlockSpec((None, None, lq, HEAD_DIM), lambda bi, kv, i: (bi, OFF_AK + kv, 0, 0)),
                    pl.BlockSpec((None, None, HEAD_DIM, lq), lambda bi, kv, i: (bi, kv, 0, 0))]
        args = [lat_kv[0], *ctx_kv, *lat_kv]
        out_rows = (batch, lq)
    else:
        q_map = lambda bi, kv, i: (0, q_head + kv, bi * n_q + i, 0)
        o_map = lambda bi, kv, i: (0, kv, bi * n_q + i, 0)
        in_specs = [pl.BlockSpec((None, ATT_GROUP, tq, HEAD_DIM), q_map), ctx_k, ctx_vt]
        args = [ctx_kv[0], *ctx_kv]
        out_rows = (1, batch * lq)
    return pl.pallas_call(
        functools.partial(_attn_kernel, with_lat=with_lat),
        grid=(batch, ATT_KV_HEADS, n_q),
        in_specs=in_specs,
        out_specs=pl.BlockSpec((None, ATT_GROUP, tq, HEAD_DIM), o_map),
        out_shape=jax.ShapeDtypeStruct((out_rows[0], ATT_HEADS, out_rows[1], HEAD_DIM), BF16),
        scratch_shapes=[pltpu.VMEM((2, n_keys, ATT_GROUP * rt), F32),
                        pltpu.VMEM((2, n_keys, ATT_GROUP * rt), BF16)],
        compiler_params=_params(3),
        name="attn_lat" if with_lat else "attn_ctx",
    )(*args)


def _ffn_tail(x1, m, n2g_ref, w1_ref, w2_ref):
    h = _norm_affine(x1, n2g_ref[...] * (1.0 + m[4:5]), m[3:4]).astype(BF16)
    hidden = w1_ref.shape[1]
    th = min(1024, hidden)
    acc = None
    for j in range(hidden // th):
        a = jnp.maximum(_dot(h, w1_ref[:, j * th:(j + 1) * th]), 0.0)
        part = _dot((a * a).astype(BF16), w2_ref[j * th:(j + 1) * th, :])
        acc = part if acc is None else acc + part
    return x1 + m[5:6] * acc


def _outffn_kernel(ret_ref, att_ref, x_ref, mod_ref, n2g_ref, wo_ref, w1_ref, w2_ref, o_ref):
    m = mod_ref[...]
    tm = x_ref.shape[0]
    sub = tm // 2 if tm % (2 * CHUNK) == 0 else tm
    x1 = []
    for r0 in range(0, tm, sub):
        rows = slice(r0, r0 + sub)
        mixed = jnp.concatenate([ret_ref[h, rows, :] for h in range(ret_ref.shape[0])]
                                + [att_ref[h, rows, :] for h in range(att_ref.shape[0])], axis=1)
        x1.append(x_ref[rows, :] + m[2:3] * _dot(mixed, wo_ref[...]))
    for r0, x1_half in zip(range(0, tm, sub), x1):
        o_ref[r0:r0 + sub, :] = _ffn_tail(x1_half, m, n2g_ref, w1_ref, w2_ref)


def _outffn_call(ret, att, xs, mod, layer, mod_row, n2g, w_out, w1, w2, name):
    b, l, d = xs.shape
    tm = _row_tile(l, ROW_TILE)
    row = pl.BlockSpec((None, tm, d), lambda bi, i: (bi, i, 0))
    heads = lambda a: pl.BlockSpec((None, a.shape[1], tm, HEAD_DIM), lambda bi, i: (bi, 0, i, 0))
    return pl.pallas_call(
        _outffn_kernel,
        grid=(b, l // tm),
        in_specs=[heads(ret), heads(att), row,
                  pl.BlockSpec((None, None, 6, d), lambda bi, i: (layer, mod_row(bi), 0, 0)),
                  _resident((1, d)), _resident_layer(*w_out), _resident_layer(*w1), _resident_layer(*w2)],
        out_specs=row,
        out_shape=jax.ShapeDtypeStruct((b, l, d), F32),
        compiler_params=_params(2),
        name=name,
    )(ret, att, xs, mod, n2g.reshape(1, d), w_out[0], w1[0], w2[0])


def _odd_kernel(x_ref, mod_ref, n1g_ref, n2g_ref, win_ref, vg_ref, ws_ref, bs_ref, wout_ref,
                w1_ref, w2_ref, o_ref, uv_ref):
    m = mod_ref[...]
    tm = x_ref.shape[0]
    width = wout_ref.shape[0]
    gain = n1g_ref[...] * (1.0 + m[1:2])
    sub = 2 * CHUNK if tm % (2 * CHUNK) == 0 else CHUNK
    n_chunks = sub // CHUNK
    pre = []
    for r0 in range(0, tm, sub):
        h = _norm_affine(x_ref[r0:r0 + sub, :], gain, m[0:1]).astype(BF16)
        pre.append((_dot(h, win_ref[:, width:]), _dot(h, win_ref[:, :width])))
    for r0, (v_pre, u_pre) in zip(range(0, tm, sub), pre):
        v = _rms(_gelu_tanh(v_pre), vg_ref[...]).astype(BF16)
        u = _gelu_tanh(u_pre)
        for g in range(CM_GROUPS):
            cols = slice(g * HEAD_DIM, (g + 1) * HEAD_DIM)
            vg = jnp.concatenate([v[ch * CHUNK:(ch + 1) * CHUNK, cols] for ch in range(n_chunks)], axis=1)
            mixed = _dot(ws_ref[g], vg)
            for ch in range(n_chunks):
                rows = slice(ch * CHUNK, (ch + 1) * CHUNK)
                sv = mixed[:, ch * CHUNK:(ch + 1) * CHUNK] + bs_ref[:, cols]
                uv_ref[r0 + ch * CHUNK:r0 + (ch + 1) * CHUNK, cols] = (u[rows, cols] * sv).astype(BF16)
    x1 = x_ref[...] + m[2:3] * _dot(uv_ref[...], wout_ref[...])
    o_ref[...] = _ffn_tail(x1, m, n2g_ref, w1_ref, w2_ref)


def _odd_call(xs, mod, layer, mod_row, n1g, n2g, w_in, v_g, w_s, b_s, w_out, w1, w2, name):
    b, l, d = xs.shape
    tm = _row_tile(l, ROW_TILE)
    width = w_out[0].shape[1]
    bias = jnp.repeat(b_s.T, width // CM_GROUPS, axis=1)
    row = pl.BlockSpec((None, tm, d), lambda bi, i: (bi, i, 0))
    return pl.pallas_call(
        _odd_kernel,
        grid=(b, l // tm),
        in_specs=[row, pl.BlockSpec((None, None, 6, d), lambda bi, i: (layer, mod_row(bi), 0, 0)),
                  _resident((1, d)), _resident((1, d)), _resident_layer(*w_in), _resident((1, width)),
                  _resident_layer(*w_s), _resident(bias.shape), _resident_layer(*w_out),
                  _resident_layer(*w1), _resident_layer(*w2)],
        out_specs=row,
        out_shape=jax.ShapeDtypeStruct((b, l, d), F32),
        scratch_shapes=[pltpu.VMEM((tm, width), BF16)],
        compiler_params=_params(2),
        name=name,
    )(xs, mod, n1g.reshape(1, d), n2g.reshape(1, d), w_in[0], v_g.reshape(1, width), w_s[0], bias,
      w_out[0], w1[0], w2[0])


def _rope_tables(l):
    rows = l // GRID_W
    r = jnp.repeat(jnp.arange(rows, dtype=F32), GRID_W)
    c = jnp.tile(jnp.arange(GRID_W, dtype=F32), rows)
    n_freq = HEAD_DIM // 4
    inv = ROPE_BASE ** (-jnp.arange(n_freq, dtype=F32) / n_freq)
    ang = jnp.concatenate([r[:, None] * inv[None, :], c[:, None] * inv[None, :]], axis=-1)
    cos, sin = jnp.cos(ang), jnp.sin(ang)
    return jnp.concatenate([cos, cos], axis=-1), jnp.concatenate([-sin, sin], axis=-1)


def kernel(x, c, ctx, c_ctx, mod_w, mod_b, norm1_g, norm2_g, ab_w_in, ab_w_out, ret_decay, att_q_norm_g,
           att_k_norm_g, cm_w_in, cm_v_norm_g, cm_w_s, cm_b_s, cm_w_out, ff_w1, ff_w2):
    batch, seq, d = x.shape
    depth = mod_w.shape[0]
    assert seq % CHUNK == 0 and ctx.shape[1] % CHUNK == 0 and seq % GRID_W == 0
    assert ab_w_in.shape[2] == AB_HEADS * HEAD_DIM

    pad = (-(batch + 1)) % 8
    c_rows = jnp.concatenate([c, c_ctx[None, :], jnp.zeros((pad, d), F32)], axis=0)
    mod = _mod_call(c_rows, mod_w, mod_b).reshape(depth, batch + 1 + pad, 6, d)
    lat_row = lambda bi: bi
    ctx_row = lambda bi: batch

    rope_tabs = _rope_tables(seq)
    ff_w1, ff_w2, ab_w_in, ab_w_out, cm_w_in, cm_w_s, cm_w_out = (
        w.astype(BF16) for w in (ff_w1, ff_w2, ab_w_in, ab_w_out, cm_w_in, cm_w_s, cm_w_out))
    h_stream = ctx.reshape(1, batch * ctx.shape[1], d)
    for l in range(depth):
        last = l == depth - 1
        i = l // 2
        w1, w2 = (ff_w1, l), (ff_w2, l)
        if l % 2 == 0:
            w_in, w_out = (ab_w_in, i), (ab_w_out, i)
            log_gamma = jax.nn.log_sigmoid(ret_decay[i].astype(F32))
            p_lat = _proj_call(x, mod, l, lat_row, norm1_g[l], w_in, att_q_norm_g[i], att_k_norm_g[i], rope_tabs)
            p_ctx = _proj_call(h_stream, mod, l, ctx_row, norm1_g[l], w_in, att_q_norm_g[i], att_k_norm_g[i], None)
            ret_l, ret_c = _ret_call(p_lat[0], p_ctx[0], log_gamma)
            att_l = _attn_call(batch, p_ctx, p_lat)
            x = _outffn_call(ret_l, att_l, x, mod, l, lat_row, norm2_g[l], w_out, w1, w2, "outffn_lat")
            if not last:
                att_c = _attn_call(batch, p_ctx, None)
                h_stream = _outffn_call(ret_c, att_c, h_stream, mod, l, ctx_row, norm2_g[l], w_out, w1, w2,
                                        "outffn_ctx")
        else:
            cm = ((cm_w_in, i), cm_v_norm_g[i], (cm_w_s, i), cm_b_s[i], (cm_w_out, i), w1, w2)
            x = _odd_call(x, mod, l, lat_row, norm1_g[l], norm2_g[l], *cm, "odd_lat")
            if not last:
                h_stream = _odd_call(h_stream, mod, l, ctx_row, norm1_g[l], norm2_g[l], *cm, "odd_ctx")
    return x
```

```python
import functools
import math

import jax
import jax.numpy as jnp
from jax import lax
from jax.experimental import pallas as pl
from jax.experimental.pallas import tpu as pltpu

F32 = jnp.float32
BF16 = jnp.bfloat16

EPS = 1e-6
HEAD_DIM = 128
RET_HEADS = 4
ATT_HEADS = 4
ATT_KV_HEADS = 2
ATT_GROUP = ATT_HEADS // ATT_KV_HEADS
CHUNK = 128
CM_GROUPS = 8
GRID_W = 64
ROPE_BASE = 10000.0
OFF_RQ, OFF_RK, OFF_RV, OFF_RG = 0, RET_HEADS, 2 * RET_HEADS, 3 * RET_HEADS
OFF_AQ = 4 * RET_HEADS
OFF_AK = OFF_AQ + ATT_HEADS
OFF_AV = OFF_AK + ATT_KV_HEADS
AB_HEADS = OFF_AV + ATT_KV_HEADS

V7X_VMEM_LIMIT_BYTES = 56 * 1024 * 1024
ROW_TILE = 512
PROJ_ROW_TILE = 1024
PROJ_SUB_TILE = 256
ATT_Q_TILE = 1024
ATT_ROW_TILE = 256
ATT_KEY_TILE = 16
MOD_COL_TILE = 1536
LOG2E = math.log2(math.e)


def _params(n_axes):
    return pltpu.CompilerParams(dimension_semantics=("arbitrary",) * n_axes,
                                vmem_limit_bytes=V7X_VMEM_LIMIT_BYTES)


def _row_tile(rows, preferred):
    assert rows % CHUNK == 0
    return max(t for t in range(CHUNK, min(rows, preferred) + 1, CHUNK) if rows % t == 0)


def _resident(shape):
    return pl.BlockSpec(shape, lambda *_: (0,) * len(shape), pipeline_mode=pl.Buffered(1))


def _resident_layer(stacked, index):
    zeros = (0,) * (stacked.ndim - 1)
    return pl.BlockSpec((None,) + stacked.shape[1:], lambda *_: (index,) + zeros, pipeline_mode=pl.Buffered(1))


def _dot(a, b):
    return jnp.dot(a, b, preferred_element_type=F32)


def _dot_nt(a, b):
    return lax.dot_general(a, b, (((1,), (1,)), ((), ())), preferred_element_type=F32)


def _dot_tn(a, b):
    return lax.dot_general(a, b, (((0,), (0,)), ((), ())), preferred_element_type=F32)


def _rms(x, g):
    return x * lax.rsqrt(jnp.mean(x * x, axis=-1, keepdims=True) + EPS) * g


def _norm_affine(x, gain, shift):
    return x * lax.rsqrt(jnp.mean(x * x, axis=-1, keepdims=True) + EPS) * gain + shift


def _silu(x):
    return x / (1.0 + jnp.exp2(x * (-LOG2E)))


def _gelu_tanh(x):
    a = -2.0 * math.sqrt(2.0 / math.pi) * LOG2E
    return x / (1.0 + jnp.exp2(x * (a + (a * 0.044715) * (x * x))))


def _mod_kernel(c_ref, w_ref, b_ref, o_ref):
    s = _silu(c_ref[...]).astype(BF16)
    o_ref[...] = _dot(s, w_ref[...].astype(BF16)) + b_ref[...]


def _mod_call(c_rows, mod_w, mod_b):
    depth, d, n = mod_w.shape
    r = c_rows.shape[0]
    tn = MOD_COL_TILE if n % MOD_COL_TILE == 0 else n
    return pl.pallas_call(
        _mod_kernel,
        grid=(depth, n // tn),
        in_specs=[pl.BlockSpec((r, d), lambda l, j: (0, 0)),
                  pl.BlockSpec((None, d, tn), lambda l, j: (l, 0, j)),
                  pl.BlockSpec((None, 1, tn), lambda l, j: (l, 0, j))],
        out_specs=pl.BlockSpec((None, r, tn), lambda l, j: (l, 0, j)),
        out_shape=jax.ShapeDtypeStruct((depth, r, n), F32),
        compiler_params=_params(2),
        name="mod",
    )(c_rows, mod_w, mod_b.reshape(depth, 1, n))


def _proj_kernel(*refs, rope):
    if rope:
        x_ref, mod_ref, g_ref, w_ref, qg_ref, kg_ref, cs_ref, sn_ref, o_ref, vt_ref = refs
    else:
        x_ref, mod_ref, g_ref, w_ref, qg_ref, kg_ref, o_ref, vt_ref = refs
    m = mod_ref[...]
    gm = g_ref[...] * (1.0 + m[1:2])
    scale = HEAD_DIM ** -0.5
    qg = qg_ref[...] * (scale * LOG2E)
    kg = kg_ref[...]
    tm = x_ref.shape[0]
    sub = PROJ_SUB_TILE if tm % PROJ_SUB_TILE == 0 else tm

    for r0 in range(0, tm, sub):
        rows = slice(r0, r0 + sub)
        h = _norm_affine(x_ref[rows, :], gm, m[0:1]).astype(BF16)
        if rope:
            cs, sn = cs_ref[rows, :], sn_ref[rows, :]
            cs_k, sn_k = cs * scale, sn * scale

        def rot(t, scaled=False):
            if not rope:
                return t * scale if scaled else t
            c, s = (cs_k, sn_k) if scaled else (cs, sn)
            return t * c + pltpu.roll(t, HEAD_DIM // 2, 1) * s

        def segment(first, count):
            y = _dot(h, w_ref[:, first * HEAD_DIM:(first + count) * HEAD_DIM])
            return [y[:, i * HEAD_DIM:(i + 1) * HEAD_DIM] for i in range(count)]

        for i, t in enumerate(segment(OFF_AQ, ATT_HEADS)):
            o_ref[OFF_AQ + i, rows, :] = rot(_rms(t, qg)).astype(BF16)
        for i, t in enumerate(segment(OFF_AK, ATT_KV_HEADS)):
            o_ref[OFF_AK + i, rows, :] = rot(_rms(t, kg)).astype(BF16)
        for i, t in enumerate(segment(OFF_AV, ATT_KV_HEADS)):
            vt_ref[i, :, rows] = t.T.astype(BF16)
        for i, t in enumerate(segment(OFF_RQ, RET_HEADS)):
            o_ref[OFF_RQ + i, rows, :] = rot(t).astype(BF16)
        for i, t in enumerate(segment(OFF_RK, RET_HEADS)):
            o_ref[OFF_RK + i, rows, :] = rot(t, scaled=True).astype(BF16)
        for i, t in enumerate(segment(OFF_RV, RET_HEADS)):
            o_ref[OFF_RV + i, rows, :] = t.astype(BF16)
        for i, t in enumerate(segment(OFF_RG, RET_HEADS)):
            o_ref[OFF_RG + i, rows, :] = t.astype(BF16)


def _proj_call(xs, mod, layer, mod_row, norm_g, w_in, q_g, k_g, rope_tabs):
    b, l, d = xs.shape
    tm = _row_tile(l, PROJ_ROW_TILE)
    rope = rope_tabs is not None
    in_specs = [pl.BlockSpec((None, tm, d), lambda bi, i: (bi, i, 0)),
                pl.BlockSpec((None, None, 6, d), lambda bi, i: (layer, mod_row(bi), 0, 0)),
                _resident((1, d)), _resident_layer(*w_in),
                _resident((1, HEAD_DIM)), _resident((1, HEAD_DIM))]
    args = [xs, mod, norm_g.reshape(1, d), w_in[0], q_g.reshape(1, HEAD_DIM), k_g.reshape(1, HEAD_DIM)]
    if rope:
        in_specs += [pl.BlockSpec((tm, HEAD_DIM), lambda bi, i: (i, 0))] * 2
        args += list(rope_tabs)
    return pl.pallas_call(
        functools.partial(_proj_kernel, rope=rope),
        grid=(b, l // tm),
        in_specs=in_specs,
        out_specs=[pl.BlockSpec((None, OFF_AV, tm, HEAD_DIM), lambda bi, i: (bi, 0, i, 0)),
                   pl.BlockSpec((None, ATT_KV_HEADS, HEAD_DIM, tm), lambda bi, i: (bi, 0, 0, i))],
        out_shape=[jax.ShapeDtypeStruct((b, OFF_AV, l, HEAD_DIM), BF16),
                   jax.ShapeDtypeStruct((b, ATT_KV_HEADS, HEAD_DIM, l), BF16)],
        compiler_params=_params(2),
        name="proj_lat" if rope else "proj_ctx",
    )(*args)


def _ret_kernel(lg_ref, ql_ref, kl_ref, vl_ref, gl_ref, qc_ref, kc_ref, vc_ref, gc_ref,
                ol_ref, oc_ref, kv_ref, st_ref, lhs_ref):
    c = CHUNK
    n_ctx = qc_ref.shape[0] // c
    n_lat = ql_ref.shape[0] // c
    head = pl.program_id(1)
    lgf = lg_ref[0, head]
    lgb = lg_ref[1, head]
    row = lax.broadcasted_iota(jnp.int32, (c, HEAD_DIM), 0).astype(F32)
    rel = (lax.broadcasted_iota(jnp.int32, (c, c), 0) - lax.broadcasted_iota(jnp.int32, (c, c), 1)).astype(F32)
    dmat = (jnp.where(rel >= 0, jnp.exp(jnp.maximum(rel, 0.0) * lgf), 0.0)
            + jnp.where(rel <= 0, jnp.exp(jnp.maximum(-rel, 0.0) * lgb), 0.0))
    kdf = jnp.exp((c - 1.0 - row) * lgf).astype(BF16)
    kdb = jnp.exp(row * lgb).astype(BF16)
    qdf = jnp.exp((row + 1.0) * lgf).astype(BF16)
    qdb = jnp.exp((c - row) * lgb).astype(BF16)
    cdf = jnp.exp(jnp.full((HEAD_DIM, HEAD_DIM), c, F32) * lgf)
    cdb = jnp.exp(jnp.full((HEAD_DIM, HEAD_DIM), c, F32) * lgb)

    chunks = ([(qc_ref, kc_ref, vc_ref, gc_ref, oc_ref, n) for n in range(n_ctx)]
              + [(ql_ref, kl_ref, vl_ref, gl_ref, ol_ref, n) for n in range(n_lat)])
    n_all = len(chunks)

    for t, (q_ref, k_ref, v_ref, _, _, n) in enumerate(chunks):
        rows = slice(n * c, (n + 1) * c)
        k = k_ref[rows, :]
        v = v_ref[rows, :]
        kv_ref[t] = _dot_tn(k, jnp.concatenate([v * kdf, v * kdb], axis=1))
        q = q_ref[rows, :]
        a = (_dot_nt(q, k) * dmat).astype(BF16)
        lhs_ref[t] = jnp.concatenate([a, q * qdf, q * qdb], axis=1)

    state = jnp.zeros((HEAD_DIM, HEAD_DIM), F32)
    for t in range(n_all):
        st_ref[t, :HEAD_DIM, :] = state.astype(BF16)
        state = cdf * state + kv_ref[t, :, :HEAD_DIM]
    state = jnp.zeros((HEAD_DIM, HEAD_DIM), F32)
    for t in list(reversed(range(n_ctx))) + list(reversed(range(n_ctx, n_all))):
        st_ref[t, HEAD_DIM:, :] = state.astype(BF16)
        state = cdb * state + kv_ref[t, :, HEAD_DIM:]

    for t, (_, _, v_ref, g_ref, o_ref, n) in enumerate(chunks):
        rows = slice(n * c, (n + 1) * c)
        o = _dot(lhs_ref[t], jnp.concatenate([v_ref[rows, :], st_ref[t]], axis=0))
        mu = jnp.mean(o, axis=-1, keepdims=True)
        dev = o - mu
        y = dev * lax.rsqrt(jnp.mean(dev * dev, axis=-1, keepdims=True) + EPS)
        o_ref[rows, :] = (y * _silu(g_ref[rows, :].astype(F32))).astype(BF16)


def _ret_call(p_lat, p_ctx, log_gamma):
    b, _, l, _ = p_lat.shape
    lc = p_ctx.shape[2] // b
    n_all = (l + lc) // CHUNK

    def head_spec(rows, off):
        return pl.BlockSpec((None, None, rows, HEAD_DIM), lambda bi, h: (bi, off + h, 0, 0))

    def ctx_spec(off):
        return pl.BlockSpec((None, None, lc, HEAD_DIM), lambda bi, h: (0, off + h, bi, 0))

    return pl.pallas_call(
        _ret_kernel,
        grid=(b, RET_HEADS),
        in_specs=[pl.BlockSpec(memory_space=pltpu.SMEM)]
        + [head_spec(l, off) for off in (OFF_RQ, OFF_RK, OFF_RV, OFF_RG)]
        + [ctx_spec(off) for off in (OFF_RQ, OFF_RK, OFF_RV, OFF_RG)],
        out_specs=[head_spec(l, 0), ctx_spec(0)],
        out_shape=[jax.ShapeDtypeStruct((b, RET_HEADS, l, HEAD_DIM), BF16),
                   jax.ShapeDtypeStruct((1, RET_HEADS, b * lc, HEAD_DIM), BF16)],
        scratch_shapes=[pltpu.VMEM((n_all, HEAD_DIM, 2 * HEAD_DIM), F32),
                        pltpu.VMEM((n_all, 2 * HEAD_DIM, HEAD_DIM), BF16),
                        pltpu.VMEM((n_all, CHUNK, CHUNK + 2 * HEAD_DIM), BF16)],
        compiler_params=_params(2),
        name="retention",
    )(log_gamma, p_lat, p_lat, p_lat, p_lat, p_ctx, p_ctx, p_ctx, p_ctx)


def _fold_rows(x, rows, op):
    parts = [x[i:i + rows, :] for i in range(0, x.shape[0], rows)]
    while len(parts) > 1:
        parts = [op(parts[i], parts[i + 1]) for i in range(0, len(parts) - 1, 2)] + parts[len(parts) & ~1:]
    return parts[0]


def _attn_kernel(*refs, with_lat):
    if with_lat:
        q_ref, kc_ref, vtc_ref, kl_ref, vtl_ref, o_ref, s_ref, p_ref = refs
    else:
        q_ref, kc_ref, vtc_ref, o_ref, s_ref, p_ref = refs
        kl_ref = vtl_ref = None
    tq = q_ref.shape[1]
    lc = kc_ref.shape[0]
    n_keys = s_ref.shape[1]
    rt = s_ref.shape[2] // ATT_GROUP
    kt = ATT_KEY_TILE
    sub = 8

    n_units = tq // rt
    piece = 4 * HEAD_DIM
    pieces = [(0, lc)] + [(s, min(s + piece, n_keys)) for s in range(lc, n_keys, piece)]

    def keys(ref_c, ref_l, lo, hi, axis):
        ref, lo, hi = (ref_c, lo, hi) if lo < lc else (ref_l, lo - lc, hi - lc)
        return ref[lo:hi, :] if axis == 0 else ref[:, lo:hi]

    def unit_rows(u):
        return pl.ds(u * rt, rt) if isinstance(u, int) else pl.ds(pl.multiple_of(u * rt, rt), rt)

    def scores(u, slot):
        q2 = jnp.concatenate([q_ref[g, unit_rows(u), :] for g in range(ATT_GROUP)], axis=0)
        for lo, hi in pieces:
            s_ref[slot, lo:hi, :] = _dot_nt(keys(kc_ref, kl_ref, lo, hi, 0), q2)

    def row_max(slot):
        m = None
        for start in range(0, n_keys, kt):
            t = _fold_rows(s_ref[slot, start:start + kt, :], sub, jnp.maximum)
            m = t if m is None else jnp.maximum(m, t)
        return jnp.max(m, axis=0, keepdims=True)

    def exp_sum(slot, m):
        den = None
        for start in range(0, n_keys, kt):
            p = jnp.exp2(s_ref[slot, start:start + kt, :] - m)
            t = _fold_rows(p, sub, jnp.add)
            den = t if den is None else den + t
            p_ref[slot, start:start + kt, :] = p.astype(BF16)
        return den

    def pv(u, slot, den):
        acc = None
        for lo, hi in pieces:
            part = _dot(keys(vtc_ref, vtl_ref, lo, hi, 1), p_ref[slot, lo:hi, :])
            acc = part if acc is None else acc + part
        out = acc / jnp.sum(den, axis=0, keepdims=True)
        for g in range(ATT_GROUP):
            o_ref[g, unit_rows(u), :] = out[:, g * rt:(g + 1) * rt].T.astype(BF16)

    scores(0, 0)
    den_prev = None
    for u in range(n_units):
        slot = u % 2
        if u + 1 < n_units:
            scores(u + 1, 1 - slot)
        den = exp_sum(slot, row_max(slot))
        if u > 0:
            pv(u - 1, 1 - slot, den_prev)
        den_prev = den
    pv(n_units - 1, (n_units - 1) % 2, den_prev)


def _attn_call(batch, ctx_kv, lat_kv):
    lc = ctx_kv[0].shape[2] // batch
    with_lat = lat_kv is not None
    lq = lat_kv[0].shape[2] if with_lat else lc
    tq = _row_tile(lq, ATT_Q_TILE)
    rt = _row_tile(tq, ATT_ROW_TILE)
    n_q = lq // tq
    n_keys = lc + (lq if with_lat else 0)
    q_head = OFF_AQ // ATT_GROUP

    ctx_k = pl.BlockSpec((None, None, lc, HEAD_DIM), lambda bi, kv, i: (0, OFF_AK + kv, bi, 0))
    ctx_vt = pl.BlockSpec((None, None, HEAD_DIM, lc), lambda bi, kv, i: (0, kv, 0, bi))
    if with_lat:
        q_map = lambda bi, kv, i: (bi, q_head + kv, i, 0)
        o_map = lambda bi, kv, i: (bi, kv, i, 0)
        in_specs = [pl.BlockSpec((None, ATT_GROUP, tq, HEAD_DIM), q_map), ctx_k, ctx_vt,
                    pl.BlockSpec((None, None, lq, HEAD_DIM), lambda bi, kv, i: (bi, OFF_AK + kv, 0, 0)),
                    pl.BlockSpec((None, None, HEAD_DIM, lq), lambda bi, kv, i: (bi, kv, 0, 0))]
        args = [lat_kv[0], *ctx_kv, *lat_kv]
        out_rows = (batch, lq)
    else:
        q_map = lambda bi, kv, i: (0, q_head + kv, bi * n_q + i, 0)
        o_map = lambda bi, kv, i: (0, kv, bi * n_q + i, 0)
        in_specs = [pl.BlockSpec((None, ATT_GROUP, tq, HEAD_DIM), q_map), ctx_k, ctx_vt]
        args = [ctx_kv[0], *ctx_kv]
        out_rows = (1, batch * lq)
    return pl.pallas_call(
        functools.partial(_attn_kernel, with_lat=with_lat),
        grid=(batch, ATT_KV_HEADS, n_q),
        in_specs=in_specs,
        out_specs=pl.BlockSpec((None, ATT_GROUP, tq, HEAD_DIM), o_map),
        out_shape=jax.ShapeDtypeStruct((out_rows[0], ATT_HEADS, out_rows[1], HEAD_DIM), BF16),
        scratch_shapes=[pltpu.VMEM((2, n_keys, ATT_GROUP * rt), F32),
                        pltpu.VMEM((2, n_keys, ATT_GROUP * rt), BF16)],
        compiler_params=_params(3),
        name="attn_lat" if with_lat else "attn_ctx",
    )(*args)


def _ffn_tail(x1, m, n2g_ref, w1_ref, w2_ref):
    h = _norm_affine(x1, n2g_ref[...] * (1.0 + m[4:5]), m[3:4]).astype(BF16)
    hidden = w1_ref.shape[1]
    th = min(1024, hidden)
    acc = None
    for j in range(hidden // th):
        a = jnp.maximum(_dot(h, w1_ref[:, j * th:(j + 1) * th]), 0.0)
        part = _dot((a * a).astype(BF16), w2_ref[j * th:(j + 1) * th, :])
        acc = part if acc is None else acc + part
    return x1 + m[5:6] * acc


def _outffn_kernel(ret_ref, att_ref, x_ref, mod_ref, n2g_ref, wo_ref, w1_ref, w2_ref, o_ref):
    m = mod_ref[...]
    tm = x_ref.shape[0]
    sub = tm // 2 if tm % (2 * CHUNK) == 0 else tm
    x1 = []
    for r0 in range(0, tm, sub):
        rows = slice(r0, r0 + sub)
        mixed = jnp.concatenate([ret_ref[h, rows, :] for h in range(ret_ref.shape[0])]
                                + [att_ref[h, rows, :] for h in range(att_ref.shape[0])], axis=1)
        x1.append(x_ref[rows, :] + m[2:3] * _dot(mixed, wo_ref[...]))
    for r0, x1_half in zip(range(0, tm, sub), x1):
        o_ref[r0:r0 + sub, :] = _ffn_tail(x1_half, m, n2g_ref, w1_ref, w2_ref)


def _outffn_call(ret, att, xs, mod, layer, mod_row, n2g, w_out, w1, w2, name):
    b, l, d = xs.shape
    tm = _row_tile(l, 2 * ROW_TILE)
    row = pl.BlockSpec((None, tm, d), lambda bi, i: (bi, i, 0))
    heads = lambda a: pl.BlockSpec((None, a.shape[1], tm, HEAD_DIM), lambda bi, i: (bi, 0, i, 0))
    return pl.pallas_call(
        _outffn_kernel,
        grid=(b, l // tm),
        in_specs=[heads(ret), heads(att), row,
                  pl.BlockSpec((None, None, 6, d), lambda bi, i: (layer, mod_row(bi), 0, 0)),
                  _resident((1, d)), _resident_layer(*w_out), _resident_layer(*w1), _resident_layer(*w2)],
        out_specs=row,
        out_shape=jax.ShapeDtypeStruct((b, l, d), F32),
        compiler_params=_params(2),
        name=name,
    )(ret, att, xs, mod, n2g.reshape(1, d), w_out[0], w1[0], w2[0])


def _odd_kernel(x_ref, mod_ref, n1g_ref, n2g_ref, win_ref, vg_ref, ws_ref, bs_ref, wout_ref,
                w1_ref, w2_ref, o_ref, uv_ref):
    m = mod_ref[...]
    tm = x_ref.shape[0]
    width = wout_ref.shape[0]
    gain = n1g_ref[...] * (1.0 + m[1:2])
    sub = 2 * CHUNK if tm % (2 * CHUNK) == 0 else CHUNK
    n_chunks = sub // CHUNK
    pre = []
    for r0 in range(0, tm, sub):
        h = _norm_affine(x_ref[r0:r0 + sub, :], gain, m[0:1]).astype(BF16)
        pre.append((_dot(h, win_ref[:, width:]), _dot(h, win_ref[:, :width])))
    for r0, (v_pre, u_pre) in zip(range(0, tm, sub), pre):
        v = _rms(_gelu_tanh(v_pre), vg_ref[...]).astype(BF16)
        u = _gelu_tanh(u_pre)
        for g in range(CM_GROUPS):
            cols = slice(g * HEAD_DIM, (g + 1) * HEAD_DIM)
            vg = jnp.concatenate([v[ch * CHUNK:(ch + 1) * CHUNK, cols] for ch in range(n_chunks)], axis=1)
            mixed = _dot(ws_ref[g], vg)
            for ch in range(n_chunks):
                rows = slice(ch * CHUNK, (ch + 1) * CHUNK)
                sv = mixed[:, ch * CHUNK:(ch + 1) * CHUNK] + bs_ref[:, cols]
                uv_ref[r0 + ch * CHUNK:r0 + (ch + 1) * CHUNK, cols] = (u[rows, cols] * sv).astype(BF16)
    tail = ROW_TILE if tm % ROW_TILE == 0 else tm
    x1 = [x_ref[r0:r0 + tail, :] + m[2:3] * _dot(uv_ref[r0:r0 + tail, :], wout_ref[...])
          for r0 in range(0, tm, tail)]
    for r0, x1_part in zip(range(0, tm, tail), x1):
        o_ref[r0:r0 + tail, :] = _ffn_tail(x1_part, m, n2g_ref, w1_ref, w2_ref)


def _odd_call(xs, mod, layer, mod_row, n1g, n2g, w_in, v_g, w_s, b_s, w_out, w1, w2, name):
    b, l, d = xs.shape
    tm = _row_tile(l, 2 * ROW_TILE)
    width = w_out[0].shape[1]
    bias = jnp.repeat(b_s.T, width // CM_GROUPS, axis=1)
    row = pl.BlockSpec((None, tm, d), lambda bi, i: (bi, i, 0))
    return pl.pallas_call(
        _odd_kernel,
        grid=(b, l // tm),
        in_specs=[row, pl.BlockSpec((None, None, 6, d), lambda bi, i: (layer, mod_row(bi), 0, 0)),
                  _resident((1, d)), _resident((1, d)), _resident_layer(*w_in), _resident((1, width)),
                  _resident_layer(*w_s), _resident(bias.shape), _resident_layer(*w_out),
                  _resident_layer(*w1), _resident_layer(*w2)],
        out_specs=row,
        out_shape=jax.ShapeDtypeStruct((b, l, d), F32),
        scratch_shapes=[pltpu.VMEM((tm, width), BF16)],
        compiler_params=_params(2),
        name=name,
    )(xs, mod, n1g.reshape(1, d), n2g.reshape(1, d), w_in[0], v_g.reshape(1, width), w_s[0], bias,
      w_out[0], w1[0], w2[0])


def _rope_tables(l):
    rows = l // GRID_W
    r = jnp.repeat(jnp.arange(rows, dtype=F32), GRID_W)
    c = jnp.tile(jnp.arange(GRID_W, dtype=F32), rows)
    n_freq = HEAD_DIM // 4
    inv = ROPE_BASE ** (-jnp.arange(n_freq, dtype=F32) / n_freq)
    ang = jnp.concatenate([r[:, None] * inv[None, :], c[:, None] * inv[None, :]], axis=-1)
    cos, sin = jnp.cos(ang), jnp.sin(ang)
    return jnp.concatenate([cos, cos], axis=-1), jnp.concatenate([-sin, sin], axis=-1)


def kernel(x, c, ctx, c_ctx, mod_w, mod_b, norm1_g, norm2_g, ab_w_in, ab_w_out, ret_decay, att_q_norm_g,
           att_k_norm_g, cm_w_in, cm_v_norm_g, cm_w_s, cm_b_s, cm_w_out, ff_w1, ff_w2):
    batch, seq, d = x.shape
    depth = mod_w.shape[0]
    assert seq % CHUNK == 0 and ctx.shape[1] % CHUNK == 0 and seq % GRID_W == 0
    assert ab_w_in.shape[2] == AB_HEADS * HEAD_DIM

    pad = (-(batch + 1)) % 8
    c_rows = jnp.concatenate([c, c_ctx[None, :], jnp.zeros((pad, d), F32)], axis=0)
    mod = _mod_call(c_rows, mod_w, mod_b).reshape(depth, batch + 1 + pad, 6, d)
    lat_row = lambda bi: bi
    ctx_row = lambda bi: batch

    rope_tabs = _rope_tables(seq)
    ff_w1, ff_w2, ab_w_in, ab_w_out, cm_w_in, cm_w_s, cm_w_out = (
        w.astype(BF16) for w in (ff_w1, ff_w2, ab_w_in, ab_w_out, cm_w_in, cm_w_s, cm_w_out))
    h_stream = ctx.reshape(1, batch * ctx.shape[1], d)
    for l in range(depth):
        last = l == depth - 1
        i = l // 2
        w1, w2 = (ff_w1, l), (ff_w2, l)
        if l % 2 == 0:
            w_in, w_out = (ab_w_in, i), (ab_w_out, i)
            log_gamma = jax.nn.log_sigmoid(ret_decay[i].astype(F32))
            p_lat = _proj_call(x, mod, l, lat_row, norm1_g[l], w_in, att_q_norm_g[i], att_k_norm_g[i], rope_tabs)
            p_ctx = _proj_call(h_stream, mod, l, ctx_row, norm1_g[l], w_in, att_q_norm_g[i], att_k_norm_g[i], None)
            ret_l, ret_c = _ret_call(p_lat[0], p_ctx[0], log_gamma)
            att_l = _attn_call(batch, p_ctx, p_lat)
            x = _outffn_call(ret_l, att_l, x, mod, l, lat_row, norm2_g[l], w_out, w1, w2, "outffn_lat")
            if not last:
                att_c = _attn_call(batch, p_ctx, None)
                h_stream = _outffn_call(ret_c, att_c, h_stream, mod, l, ctx_row, norm2_g[l], w_out, w1, w2,
                                        "outffn_ctx")
        else:
            cm = ((cm_w_in, i), cm_v_norm_g[i], (cm_w_s, i), cm_b_s[i], (cm_w_out, i), w1, w2)
            x = _odd_call(x, mod, l, lat_row, norm1_g[l], norm2_g[l], *cm, "odd_lat")
            if not last:
                h_stream = _odd_call(h_stream, mod, l, ctx_row, norm1_g[l], norm2_g[l], *cm, "odd_ctx")
    return x
```

```python
import functools
import math

import jax
import jax.numpy as jnp
from jax import lax
from jax.experimental import pallas as pl
from jax.experimental.pallas import tpu as pltpu

F32 = jnp.float32
BF16 = jnp.bfloat16

EPS = 1e-6
HEAD_DIM = 128
RET_HEADS = 4
RET_HEADS_PER_STEP = 2
ATT_HEADS = 4
ATT_KV_HEADS = 2
ATT_GROUP = ATT_HEADS // ATT_KV_HEADS
CHUNK = 128
CM_GROUPS = 8
GRID_W = 64
ROPE_BASE = 10000.0
OFF_RQ, OFF_RK, OFF_RV, OFF_RG = 0, RET_HEADS, 2 * RET_HEADS, 3 * RET_HEADS
OFF_AQ = 4 * RET_HEADS
OFF_AK = OFF_AQ + ATT_HEADS
OFF_AV = OFF_AK + ATT_KV_HEADS
AB_HEADS = OFF_AV + ATT_KV_HEADS

V7X_VMEM_LIMIT_BYTES = 56 * 1024 * 1024
ROW_TILE = 512
PROJ_ROW_TILE = 1024
PROJ_SUB_TILE = 256
ATT_Q_TILE = 1024
ATT_ROW_TILE = 256
ATT_KEY_TILE = 16
MOD_COL_TILE = 1536
LOG2E = math.log2(math.e)


def _params(n_axes):
    return pltpu.CompilerParams(dimension_semantics=("arbitrary",) * n_axes,
                                vmem_limit_bytes=V7X_VMEM_LIMIT_BYTES)


def _row_tile(rows, preferred):
    assert rows % CHUNK == 0
    return max(t for t in range(CHUNK, min(rows, preferred) + 1, CHUNK) if rows % t == 0)


def _resident(shape):
    return pl.BlockSpec(shape, lambda *_: (0,) * len(shape), pipeline_mode=pl.Buffered(1))


def _resident_layer(stacked, index):
    zeros = (0,) * (stacked.ndim - 1)
    return pl.BlockSpec((None,) + stacked.shape[1:], lambda *_: (index,) + zeros, pipeline_mode=pl.Buffered(1))


def _dot(a, b):
    return jnp.dot(a, b, preferred_element_type=F32)


def _dot_nt(a, b):
    return lax.dot_general(a, b, (((1,), (1,)), ((), ())), preferred_element_type=F32)


def _dot_tn(a, b):
    return lax.dot_general(a, b, (((0,), (0,)), ((), ())), preferred_element_type=F32)


def _rms(x, g):
    return x * lax.rsqrt(jnp.mean(x * x, axis=-1, keepdims=True) + EPS) * g


def _norm_affine(x, gain, shift):
    return x * lax.rsqrt(jnp.mean(x * x, axis=-1, keepdims=True) + EPS) * gain + shift


def _silu(x):
    return x / (1.0 + jnp.exp2(x * (-LOG2E)))


def _gelu_tanh(x):
    a = -2.0 * math.sqrt(2.0 / math.pi) * LOG2E
    return x / (1.0 + jnp.exp2(x * (a + (a * 0.044715) * (x * x))))


def _mod_kernel(c_ref, w_ref, b_ref, o_ref):
    s = _silu(c_ref[...]).astype(BF16)
    o_ref[...] = _dot(s, w_ref[...].astype(BF16)) + b_ref[...]


def _mod_call(c_rows, mod_w, mod_b):
    depth, d, n = mod_w.shape
    r = c_rows.shape[0]
    tn = MOD_COL_TILE if n % MOD_COL_TILE == 0 else n
    return pl.pallas_call(
        _mod_kernel,
        grid=(depth, n // tn),
        in_specs=[pl.BlockSpec((r, d), lambda l, j: (0, 0)),
                  pl.BlockSpec((None, d, tn), lambda l, j: (l, 0, j)),
                  pl.BlockSpec((None, 1, tn), lambda l, j: (l, 0, j))],
        out_specs=pl.BlockSpec((None, r, tn), lambda l, j: (l, 0, j)),
        out_shape=jax.ShapeDtypeStruct((depth, r, n), F32),
        compiler_params=_params(2),
        name="mod",
    )(c_rows, mod_w, mod_b.reshape(depth, 1, n))


def _proj_kernel(*refs, rope):
    if rope:
        x_ref, mod_ref, g_ref, w_ref, qg_ref, kg_ref, cs_ref, sn_ref, o_ref, vt_ref = refs
    else:
        x_ref, mod_ref, g_ref, w_ref, qg_ref, kg_ref, o_ref, vt_ref = refs
    m = mod_ref[...]
    gm = g_ref[...] * (1.0 + m[1:2])
    scale = HEAD_DIM ** -0.5
    qg = qg_ref[...] * (scale * LOG2E)
    kg = kg_ref[...]
    tm = x_ref.shape[0]
    sub = PROJ_SUB_TILE if tm % PROJ_SUB_TILE == 0 else tm

    for r0 in range(0, tm, sub):
        rows = slice(r0, r0 + sub)
        h = _norm_affine(x_ref[rows, :], gm, m[0:1]).astype(BF16)
        if rope:
            cs, sn = cs_ref[rows, :], sn_ref[rows, :]
            cs_k, sn_k = cs * scale, sn * scale

        def rot(t, scaled=False):
            if not rope:
                return t * scale if scaled else t
            c, s = (cs_k, sn_k) if scaled else (cs, sn)
            return t * c + pltpu.roll(t, HEAD_DIM // 2, 1) * s

        def segment(first, count):
            y = _dot(h, w_ref[:, first * HEAD_DIM:(first + count) * HEAD_DIM])
            return [y[:, i * HEAD_DIM:(i + 1) * HEAD_DIM] for i in range(count)]

        for i, t in enumerate(segment(OFF_AQ, ATT_HEADS)):
            o_ref[OFF_AQ + i, rows, :] = rot(_rms(t, qg)).astype(BF16)
        for i, t in enumerate(segment(OFF_AK, ATT_KV_HEADS)):
            o_ref[OFF_AK + i, rows, :] = rot(_rms(t, kg)).astype(BF16)
        for i, t in enumerate(segment(OFF_AV, ATT_KV_HEADS)):
            vt_ref[i, :, rows] = t.T.astype(BF16)
        for i, t in enumerate(segment(OFF_RQ, RET_HEADS)):
            o_ref[OFF_RQ + i, rows, :] = rot(t).astype(BF16)
        for i, t in enumerate(segment(OFF_RK, RET_HEADS)):
            o_ref[OFF_RK + i, rows, :] = rot(t, scaled=True).astype(BF16)
        for i, t in enumerate(segment(OFF_RV, RET_HEADS)):
            o_ref[OFF_RV + i, rows, :] = t.astype(BF16)
        for i, t in enumerate(segment(OFF_RG, RET_HEADS)):
            o_ref[OFF_RG + i, rows, :] = t.astype(BF16)


def _proj_call(xs, mod, layer, mod_row, norm_g, w_in, q_g, k_g, rope_tabs):
    b, l, d = xs.shape
    tm = _row_tile(l, PROJ_ROW_TILE)
    rope = rope_tabs is not None
    in_specs = [pl.BlockSpec((None, tm, d), lambda bi, i: (bi, i, 0)),
                pl.BlockSpec((None, None, 6, d), lambda bi, i: (layer, mod_row(bi), 0, 0)),
                _resident((1, d)), _resident_layer(*w_in),
                _resident((1, HEAD_DIM)), _resident((1, HEAD_DIM))]
    args = [xs, mod, norm_g.reshape(1, d), w_in[0], q_g.reshape(1, HEAD_DIM), k_g.reshape(1, HEAD_DIM)]
    if rope:
        in_specs += [pl.BlockSpec((tm, HEAD_DIM), lambda bi, i: (i, 0))] * 2
        args += list(rope_tabs)
    return pl.pallas_call(
        functools.partial(_proj_kernel, rope=rope),
        grid=(b, l // tm),
        in_specs=in_specs,
        out_specs=[pl.BlockSpec((None, OFF_AV, tm, HEAD_DIM), lambda bi, i: (bi, 0, i, 0)),
                   pl.BlockSpec((None, ATT_KV_HEADS, HEAD_DIM, tm), lambda bi, i: (bi, 0, 0, i))],
        out_shape=[jax.ShapeDtypeStruct((b, OFF_AV, l, HEAD_DIM), BF16),
                   jax.ShapeDtypeStruct((b, ATT_KV_HEADS, HEAD_DIM, l), BF16)],
        compiler_params=_params(2),
        name="proj_lat" if rope else "proj_ctx",
    )(*args)


def _ret_kernel(lg_ref, *refs):
    heads_per_step = refs[0].shape[0]
    for hp in range(heads_per_step):
        _ret_head(pl.program_id(1) * heads_per_step + hp, lg_ref, *[r.at[hp] for r in refs])


def _ret_head(head, lg_ref, ql_ref, kl_ref, vl_ref, gl_ref, qc_ref, kc_ref, vc_ref, gc_ref,
              ol_ref, oc_ref, kv_ref, st_ref, lhs_ref):
    c = CHUNK
    n_ctx = qc_ref.shape[0] // c
    n_lat = ql_ref.shape[0] // c
    lgf = lg_ref[0, head]
    lgb = lg_ref[1, head]
    row = lax.broadcasted_iota(jnp.int32, (c, HEAD_DIM), 0).astype(F32)
    rel = (lax.broadcasted_iota(jnp.int32, (c, c), 0) - lax.broadcasted_iota(jnp.int32, (c, c), 1)).astype(F32)
    dmat = (jnp.where(rel >= 0, jnp.exp(jnp.maximum(rel, 0.0) * lgf), 0.0)
            + jnp.where(rel <= 0, jnp.exp(jnp.maximum(-rel, 0.0) * lgb), 0.0))
    kdf = jnp.exp((c - 1.0 - row) * lgf).astype(BF16)
    kdb = jnp.exp(row * lgb).astype(BF16)
    qdf = jnp.exp((row + 1.0) * lgf).astype(BF16)
    qdb = jnp.exp((c - row) * lgb).astype(BF16)
    cdf = jnp.exp(jnp.full((HEAD_DIM, HEAD_DIM), c, F32) * lgf)
    cdb = jnp.exp(jnp.full((HEAD_DIM, HEAD_DIM), c, F32) * lgb)

    chunks = ([(qc_ref, kc_ref, vc_ref, gc_ref, oc_ref, n) for n in range(n_ctx)]
              + [(ql_ref, kl_ref, vl_ref, gl_ref, ol_ref, n) for n in range(n_lat)])
    n_all = len(chunks)

    for t, (q_ref, k_ref, v_ref, _, _, n) in enumerate(chunks):
        rows = slice(n * c, (n + 1) * c)
        k = k_ref[rows, :]
        v = v_ref[rows, :]
        kv_ref[t] = _dot_tn(k, jnp.concatenate([v * kdf, v * kdb], axis=1))
        q = q_ref[rows, :]
        a = (_dot_nt(q, k) * dmat).astype(BF16)
        lhs_ref[t] = jnp.concatenate([a, q * qdf, q * qdb], axis=1)

    state = jnp.zeros((HEAD_DIM, HEAD_DIM), F32)
    for t in range(n_all):
        st_ref[t, :HEAD_DIM, :] = state.astype(BF16)
        state = cdf * state + kv_ref[t, :, :HEAD_DIM]
    state = jnp.zeros((HEAD_DIM, HEAD_DIM), F32)
    for t in list(reversed(range(n_ctx))) + list(reversed(range(n_ctx, n_all))):
        st_ref[t, HEAD_DIM:, :] = state.astype(BF16)
        state = cdb * state + kv_ref[t, :, HEAD_DIM:]

    for t, (_, _, v_ref, g_ref, o_ref, n) in enumerate(chunks):
        rows = slice(n * c, (n + 1) * c)
        o = _dot(lhs_ref[t], jnp.concatenate([v_ref[rows, :], st_ref[t]], axis=0))
        mu = jnp.mean(o, axis=-1, keepdims=True)
        dev = o - mu
        y = dev * lax.rsqrt(jnp.mean(dev * dev, axis=-1, keepdims=True) + EPS)
        o_ref[rows, :] = (y * _silu(g_ref[rows, :].astype(F32))).astype(BF16)


def _ret_call(p_lat, p_ctx, log_gamma):
    b, _, l, _ = p_lat.shape
    lc = p_ctx.shape[2] // b
    n_all = (l + lc) // CHUNK

    hps = RET_HEADS_PER_STEP
    assert RET_HEADS % hps == 0

    def head_spec(rows, off):
        return pl.BlockSpec((None, hps, rows, HEAD_DIM), lambda bi, h: (bi, off // hps + h, 0, 0))

    def ctx_spec(off):
        return pl.BlockSpec((None, hps, lc, HEAD_DIM), lambda bi, h: (0, off // hps + h, bi, 0))

    return pl.pallas_call(
        _ret_kernel,
        grid=(b, RET_HEADS // hps),
        in_specs=[pl.BlockSpec(memory_space=pltpu.SMEM)]
        + [head_spec(l, off) for off in (OFF_RQ, OFF_RK, OFF_RV, OFF_RG)]
        + [ctx_spec(off) for off in (OFF_RQ, OFF_RK, OFF_RV, OFF_RG)],
        out_specs=[head_spec(l, 0), ctx_spec(0)],
        out_shape=[jax.ShapeDtypeStruct((b, RET_HEADS, l, HEAD_DIM), BF16),
                   jax.ShapeDtypeStruct((1, RET_HEADS, b * lc, HEAD_DIM), BF16)],
        scratch_shapes=[pltpu.VMEM((hps, n_all, HEAD_DIM, 2 * HEAD_DIM), F32),
                        pltpu.VMEM((hps, n_all, 2 * HEAD_DIM, HEAD_DIM), BF16),
                        pltpu.VMEM((hps, n_all, CHUNK, CHUNK + 2 * HEAD_DIM), BF16)],
        compiler_params=_params(2),
        name="retention",
    )(log_gamma, p_lat, p_lat, p_lat, p_lat, p_ctx, p_ctx, p_ctx, p_ctx)


def _fold_rows(x, rows, op):
    parts = [x[i:i + rows, :] for i in range(0, x.shape[0], rows)]
    while len(parts) > 1:
        parts = [op(parts[i], parts[i + 1]) for i in range(0, len(parts) - 1, 2)] + parts[len(parts) & ~1:]
    return parts[0]


def _attn_kernel(*refs, with_lat):
    if with_lat:
        q_ref, kc_ref, vtc_ref, kl_ref, vtl_ref, o_ref, s_ref, p_ref = refs
    else:
        q_ref, kc_ref, vtc_ref, o_ref, s_ref, p_ref = refs
        kl_ref = vtl_ref = None
    tq = q_ref.shape[1]
    lc = kc_ref.shape[0]
    n_keys = s_ref.shape[1]
    rt = s_ref.shape[2] // ATT_GROUP
    kt = ATT_KEY_TILE
    sub = 8

    n_units = tq // rt
    piece = 4 * HEAD_DIM
    pieces = [(0, lc)] + [(s, min(s + piece, n_keys)) for s in range(lc, n_keys, piece)]

    def keys(ref_c, ref_l, lo, hi, axis):
        ref, lo, hi = (ref_c, lo, hi) if lo < lc else (ref_l, lo - lc, hi - lc)
        return ref[lo:hi, :] if axis == 0 else ref[:, lo:hi]

    def unit_rows(u):
        return pl.ds(u * rt, rt) if isinstance(u, int) else pl.ds(pl.multiple_of(u * rt, rt), rt)

    def scores(u, slot):
        q2 = jnp.concatenate([q_ref[g, unit_rows(u), :] for g in range(ATT_GROUP)], axis=0)
        for lo, hi in pieces:
            s_ref[slot, lo:hi, :] = _dot_nt(keys(kc_ref, kl_ref, lo, hi, 0), q2)

    def row_max(slot):
        m = None
        for start in range(0, n_keys, kt):
            t = _fold_rows(s_ref[slot, start:start + kt, :], sub, jnp.maximum)
            m = t if m is None else jnp.maximum(m, t)
        return jnp.max(m, axis=0, keepdims=True)

    def exp_sum(slot, m):
        den = None
        for start in range(0, n_keys, kt):
            p = jnp.exp2(s_ref[slot, start:start + kt, :] - m)
            t = _fold_rows(p, sub, jnp.add)
            den = t if den is None else den + t
            p_ref[slot, start:start + kt, :] = p.astype(BF16)
        return den

    def pv(u, slot, den):
        acc = None
        for lo, hi in pieces:
            part = _dot(keys(vtc_ref, vtl_ref, lo, hi, 1), p_ref[slot, lo:hi, :])
            acc = part if acc is None else acc + part
        out = acc / jnp.sum(den, axis=0, keepdims=True)
        for g in range(ATT_GROUP):
            o_ref[g, unit_rows(u), :] = out[:, g * rt:(g + 1) * rt].T.astype(BF16)

    scores(0, 0)
    den_prev = None
    for u in range(n_units):
        slot = u % 2
        if u + 1 < n_units:
            scores(u + 1, 1 - slot)
        den = exp_sum(slot, row_max(slot))
        if u > 0:
            pv(u - 1, 1 - slot, den_prev)
        den_prev = den
    pv(n_units - 1, (n_units - 1) % 2, den_prev)


def _attn_call(batch, ctx_kv, lat_kv):
    lc = ctx_kv[0].shape[2] // batch
    with_lat = lat_kv is not None
    lq = lat_kv[0].shape[2] if with_lat else lc
    tq = _row_tile(lq, ATT_Q_TILE)
    rt = _row_tile(tq, ATT_ROW_TILE)
    n_q = lq // tq
    n_keys = lc + (lq if with_lat else 0)
    q_head = OFF_AQ // ATT_GROUP

    ctx_k = pl.BlockSpec((None, None, lc, HEAD_DIM), lambda bi, kv, i: (0, OFF_AK + kv, bi, 0))
    ctx_vt = pl.BlockSpec((None, None, HEAD_DIM, lc), lambda bi, kv, i: (0, kv, 0, bi))
    if with_lat:
        q_map = lambda bi, kv, i: (bi, q_head + kv, i, 0)
        o_map = lambda bi, kv, i: (bi, kv, i, 0)
        in_specs = [pl.BlockSpec((None, ATT_GROUP, tq, HEAD_DIM), q_map), ctx_k, ctx_vt,
                    pl.BlockSpec((None, None, lq, HEAD_DIM), lambda bi, kv, i: (bi, OFF_AK + kv, 0, 0)),
                    pl.BlockSpec((None, None, HEAD_DIM, lq), lambda bi, kv, i: (bi, kv, 0, 0))]
        args = [lat_kv[0], *ctx_kv, *lat_kv]
        out_rows = (batch, lq)
    else:
        q_map = lambda bi, kv, i: (0, q_head + kv, bi * n_q + i, 0)
        o_map = lambda bi, kv, i: (0, kv, bi * n_q + i, 0)
        in_specs = [pl.BlockSpec((None, ATT_GROUP, tq, HEAD_DIM), q_map), ctx_k, ctx_vt]
        args = [ctx_kv[0], *ctx_kv]
        out_rows = (1, batch * lq)
    return pl.pallas_call(
        functools.partial(_attn_kernel, with_lat=with_lat),
        grid=(batch, ATT_KV_HEADS, n_q),
        in_specs=in_specs,
        out_specs=pl.BlockSpec((None, ATT_GROUP, tq, HEAD_DIM), o_map),
        out_shape=jax.ShapeDtypeStruct((out_rows[0], ATT_HEADS, out_rows[1], HEAD_DIM), BF16),
        scratch_shapes=[pltpu.VMEM((2, n_keys, ATT_GROUP * rt), F32),
                        pltpu.VMEM((2, n_keys, ATT_GROUP * rt), BF16)],
        compiler_params=_params(3),
        name="attn_lat" if with_lat else "attn_ctx",
    )(*args)


def _ffn_tail(x1, m, n2g_ref, w1_ref, w2_ref):
    h = _norm_affine(x1, n2g_ref[...] * (1.0 + m[4:5]), m[3:4]).astype(BF16)
    hidden = w1_ref.shape[1]
    th = min(1024, hidden)
    acc = None
    for j in range(hidden // th):
        a = jnp.maximum(_dot(h, w1_ref[:, j * th:(j + 1) * th]), 0.0)
        part = _dot((a * a).astype(BF16), w2_ref[j * th:(j + 1) * th, :])
        acc = part if acc is None else acc + part
    return x1 + m[5:6] * acc


def _outffn_kernel(ret_ref, att_ref, x_ref, mod_ref, n2g_ref, wo_ref, w1_ref, w2_ref, o_ref):
    m = mod_ref[...]
    tm = x_ref.shape[0]
    sub = tm // 2 if tm % (2 * CHUNK) == 0 else tm
    x1 = []
    for r0 in range(0, tm, sub):
        rows = slice(r0, r0 + sub)
        mixed = jnp.concatenate([ret_ref[h, rows, :] for h in range(ret_ref.shape[0])]
                                + [att_ref[h, rows, :] for h in range(att_ref.shape[0])], axis=1)
        x1.append(x_ref[rows, :] + m[2:3] * _dot(mixed, wo_ref[...]))
    for r0, x1_half in zip(range(0, tm, sub), x1):
        o_ref[r0:r0 + sub, :] = _ffn_tail(x1_half, m, n2g_ref, w1_ref, w2_ref)


def _outffn_call(ret, att, xs, mod, layer, mod_row, n2g, w_out, w1, w2, name):
    b, l, d = xs.shape
    tm = _row_tile(l, 2 * ROW_TILE)
    row = pl.BlockSpec((None, tm, d), lambda bi, i: (bi, i, 0))
    heads = lambda a: pl.BlockSpec((None, a.shape[1], tm, HEAD_DIM), lambda bi, i: (bi, 0, i, 0))
    return pl.pallas_call(
        _outffn_kernel,
        grid=(b, l // tm),
        in_specs=[heads(ret), heads(att), row,
                  pl.BlockSpec((None, None, 6, d), lambda bi, i: (layer, mod_row(bi), 0, 0)),
                  _resident((1, d)), _resident_layer(*w_out), _resident_layer(*w1), _resident_layer(*w2)],
        out_specs=row,
        out_shape=jax.ShapeDtypeStruct((b, l, d), F32),
        compiler_params=_params(2),
        name=name,
    )(ret, att, xs, mod, n2g.reshape(1, d), w_out[0], w1[0], w2[0])


def _odd_kernel(x_ref, mod_ref, n1g_ref, n2g_ref, win_ref, vg_ref, ws_ref, bs_ref, wout_ref,
                w1_ref, w2_ref, o_ref, uv_ref):
    m = mod_ref[...]
    tm = x_ref.shape[0]
    width = wout_ref.shape[0]
    gain = n1g_ref[...] * (1.0 + m[1:2])
    sub = 2 * CHUNK if tm % (2 * CHUNK) == 0 else CHUNK
    n_chunks = sub // CHUNK
    pre = []
    for r0 in range(0, tm, sub):
        h = _norm_affine(x_ref[r0:r0 + sub, :], gain, m[0:1]).astype(BF16)
        pre.append((_dot(h, win_ref[:, width:]), _dot(h, win_ref[:, :width])))
    for r0, (v_pre, u_pre) in zip(range(0, tm, sub), pre):
        v = _rms(_gelu_tanh(v_pre), vg_ref[...]).astype(BF16)
        u = _gelu_tanh(u_pre)
        for g in range(CM_GROUPS):
            cols = slice(g * HEAD_DIM, (g + 1) * HEAD_DIM)
            vg = jnp.concatenate([v[ch * CHUNK:(ch + 1) * CHUNK, cols] for ch in range(n_chunks)], axis=1)
            mixed = _dot(ws_ref[g], vg)
            for ch in range(n_chunks):
                rows = slice(ch * CHUNK, (ch + 1) * CHUNK)
                sv = mixed[:, ch * CHUNK:(ch + 1) * CHUNK] + bs_ref[:, cols]
                uv_ref[r0 + ch * CHUNK:r0 + (ch + 1) * CHUNK, cols] = (u[rows, cols] * sv).astype(BF16)
    tail = ROW_TILE if tm % ROW_TILE == 0 else tm
    x1 = [x_ref[r0:r0 + tail, :] + m[2:3] * _dot(uv_ref[r0:r0 + tail, :], wout_ref[...])
          for r0 in range(0, tm, tail)]
    for r0, x1_part in zip(range(0, tm, tail), x1):
        o_ref[r0:r0 + tail, :] = _ffn_tail(x1_part, m, n2g_ref, w1_ref, w2_ref)


def _odd_call(xs, mod, layer, mod_row, n1g, n2g, w_in, v_g, w_s, b_s, w_out, w1, w2, name):
    b, l, d = xs.shape
    tm = _row_tile(l, 2 * ROW_TILE)
    width = w_out[0].shape[1]
    bias = jnp.repeat(b_s.T, width // CM_GROUPS, axis=1)
    row = pl.BlockSpec((None, tm, d), lambda bi, i: (bi, i, 0))
    return pl.pallas_call(
        _odd_kernel,
        grid=(b, l // tm),
        in_specs=[row, pl.BlockSpec((None, None, 6, d), lambda bi, i: (layer, mod_row(bi), 0, 0)),
                  _resident((1, d)), _resident((1, d)), _resident_layer(*w_in), _resident((1, width)),
                  _resident_layer(*w_s), _resident(bias.shape), _resident_layer(*w_out),
                  _resident_layer(*w1), _resident_layer(*w2)],
        out_specs=row,
        out_shape=jax.ShapeDtypeStruct((b, l, d), F32),
        scratch_shapes=[pltpu.VMEM((tm, width), BF16)],
        compiler_params=_params(2),
        name=name,
    )(xs, mod, n1g.reshape(1, d), n2g.reshape(1, d), w_in[0], v_g.reshape(1, width), w_s[0], bias,
      w_out[0], w1[0], w2[0])


def _rope_tables(l):
    rows = l // GRID_W
    r = jnp.repeat(jnp.arange(rows, dtype=F32), GRID_W)
    c = jnp.tile(jnp.arange(GRID_W, dtype=F32), rows)
    n_freq = HEAD_DIM // 4
    inv = ROPE_BASE ** (-jnp.arange(n_freq, dtype=F32) / n_freq)
    ang = jnp.concatenate([r[:, None] * inv[None, :], c[:, None] * inv[None, :]], axis=-1)
    cos, sin = jnp.cos(ang), jnp.sin(ang)
    return jnp.concatenate([cos, cos], axis=-1), jnp.concatenate([-sin, sin], axis=-1)


def kernel(x, c, ctx, c_ctx, mod_w, mod_b, norm1_g, norm2_g, ab_w_in, ab_w_out, ret_decay, att_q_norm_g,
           att_k_norm_g, cm_w_in, cm_v_norm_g, cm_w_s, cm_b_s, cm_w_out, ff_w1, ff_w2):
    batch, seq, d = x.shape
    depth = mod_w.shape[0]
    assert seq % CHUNK == 0 and ctx.shape[1] % CHUNK == 0 and seq % GRID_W == 0
    assert ab_w_in.shape[2] == AB_HEADS * HEAD_DIM

    pad = (-(batch + 1)) % 8
    c_rows = jnp.concatenate([c, c_ctx[None, :], jnp.zeros((pad, d), F32)], axis=0)
    mod = _mod_call(c_rows, mod_w, mod_b).reshape(depth, batch + 1 + pad, 6, d)
    lat_row = lambda bi: bi
    ctx_row = lambda bi: batch

    rope_tabs = _rope_tables(seq)
    ff_w1, ff_w2, ab_w_in, ab_w_out, cm_w_in, cm_w_s, cm_w_out = (
        w.astype(BF16) for w in (ff_w1, ff_w2, ab_w_in, ab_w_out, cm_w_in, cm_w_s, cm_w_out))
    h_stream = ctx.reshape(1, batch * ctx.shape[1], d)
    for l in range(depth):
        last = l == depth - 1
        i = l // 2
        w1, w2 = (ff_w1, l), (ff_w2, l)
        if l % 2 == 0:
            w_in, w_out = (ab_w_in, i), (ab_w_out, i)
            log_gamma = jax.nn.log_sigmoid(ret_decay[i].astype(F32))
            p_lat = _proj_call(x, mod, l, lat_row, norm1_g[l], w_in, att_q_norm_g[i], att_k_norm_g[i], rope_tabs)
            p_ctx = _proj_call(h_stream, mod, l, ctx_row, norm1_g[l], w_in, att_q_norm_g[i], att_k_norm_g[i], None)
            ret_l, ret_c = _ret_call(p_lat[0], p_ctx[0], log_gamma)
            att_l = _attn_call(batch, p_ctx, p_lat)
            x = _outffn_call(ret_l, att_l, x, mod, l, lat_row, norm2_g[l], w_out, w1, w2, "outffn_lat")
            if not last:
                att_c = _attn_call(batch, p_ctx, None)
                h_stream = _outffn_call(ret_c, att_c, h_stream, mod, l, ctx_row, norm2_g[l], w_out, w1, w2,
                                        "outffn_ctx")
        else:
            cm = ((cm_w_in, i), cm_v_norm_g[i], (cm_w_s, i), cm_b_s[i], (cm_w_out, i), w1, w2)
            x = _odd_call(x, mod, l, lat_row, norm1_g[l], norm2_g[l], *cm, "odd_lat")
            if not last:
                h_stream = _odd_call(h_stream, mod, l, ctx_row, norm1_g[l], norm2_g[l], *cm, "odd_ctx")
    return x
```
